```python
import math
import jax, jax.numpy as jnp
from jax import lax
import numpy as np

D_MODEL = 1024
BATCH = 2
SEQ = 8192
DEPTH = 4
DEC_BATCH = 128
DEC_SEQ = 1
PAST_LEN = 8192
PAGE_SIZE = 128

N_MIXERS = 2
N_A = (DEPTH + 1) // 2
N_B = DEPTH // 2
D_RNN = D_MODEL
LRU_BLOCKS = 4
LRU_BW = D_RNN // LRU_BLOCKS
LRU_C = 8.0
CONV_W = 4
N_HEADS = 16
N_KV = 4
GROUP = N_HEADS // N_KV
HEAD_DIM = 64
Q_DIM = N_HEADS * HEAD_DIM
KV_DIM = N_KV * HEAD_DIM
QKV_DIM = Q_DIM + 2 * KV_DIM
WINDOW = 128
BLOCK = WINDOW
D_FF = 4 * D_MODEL
RMS_EPS = 1e-6
NEG_INF = -1e30

kernel_name = "hybrid_rglru_swa_sink_decoder_step"


def rms_norm(x, g):
    xf = x.astype(jnp.float32)
    y = xf * lax.rsqrt(jnp.mean(xf * xf, axis=-1, keepdims=True) + RMS_EPS)
    return (y * g.astype(jnp.float32)).astype(x.dtype)


def causal_conv(x, prev, w, b):
    t = x.shape[1]
    xp = jnp.concatenate([prev.astype(x.dtype), x], axis=1)
    y = xp[:, 0:t] * w[0]
    for k in range(1, CONV_W):
        y = y + xp[:, k:k + t] * w[k]
    return y + b, xp[:, t:]


def rg_lru(x, h0, w_a, b_a, w_x, b_x, lam):
    n, t, _ = x.shape
    f32 = jnp.float32
    xf = x.astype(f32)
    xb = xf.reshape(n, t, LRU_BLOCKS, LRU_BW)
    r = jax.nn.sigmoid(jnp.einsum('ntkb,kbc->ntkc', xb, w_a.astype(f32)).reshape(n, t, D_RNN) + b_a.astype(f32))
    i = jax.nn.sigmoid(jnp.einsum('ntkb,kbc->ntkc', xb, w_x.astype(f32)).reshape(n, t, D_RNN) + b_x.astype(f32))
    log_a = -LRU_C * r * jax.nn.softplus(-lam.astype(f32))
    a = jnp.exp(log_a)
    u = jnp.sqrt(-jnp.expm1(2.0 * log_a)) * (i * xf)

    def step(h, au):
        a_t, u_t = au
        h = a_t * h + u_t
        return h, h

    h_last, hs = lax.scan(step, h0.astype(f32), (a.swapaxes(0, 1), u.swapaxes(0, 1)))
    return hs.swapaxes(0, 1).astype(x.dtype), h_last.astype(h0.dtype)


def recurrent_block(x, h0, conv_prev, w_in, conv_w, conv_b, w_a, b_a, w_x, b_x, lam, w_out):
    u = x @ w_in
    gate, xr = jnp.split(u, 2, axis=-1)
    xc, conv_new = causal_conv(xr, conv_prev, conv_w, conv_b)
    y, h_new = rg_lru(xc, h0, w_a, b_a, w_x, b_x, lam)
    return (y * jax.nn.gelu(gate)) @ w_out, h_new, conv_new


def qkv_heads(x, w_qkv, q_g, k_g):
    n, t, _ = x.shape
    q, k, v = jnp.split(x @ w_qkv, [Q_DIM, Q_DIM + KV_DIM], axis=-1)
    q = rms_norm(q.reshape(n, t, N_KV, GROUP, HEAD_DIM), q_g)
    k = rms_norm(k.reshape(n, t, N_KV, HEAD_DIM), k_g)
    v = v.reshape(n, t, N_KV, HEAD_DIM)
    return q, k, v


def alibi_slopes():
    h = jnp.arange(1, N_HEADS + 1, dtype=jnp.float32)
    return jnp.exp2(-8.0 * h / N_HEADS).reshape(N_KV, GROUP)


def sink_attention(q, k, v, dist, mask, sinks):
    s = jnp.einsum('...qkgd,...skd->...kgqs', q, k).astype(jnp.float32) * (HEAD_DIM ** -0.5)
    s = s - alibi_slopes()[:, :, None, None] * dist.astype(jnp.float32)
    s = jnp.where(mask, s, NEG_INF)
    sink = sinks.astype(jnp.float32).reshape(N_KV, GROUP)[:, :, None, None]
    m = jnp.maximum(jnp.max(s, axis=-1, keepdims=True), sink)
    e = jnp.exp(s - m)
    p = e / (jnp.sum(e, axis=-1, keepdims=True) + jnp.exp(sink - m))
    return jnp.einsum('...kgqs,...skd->...qkgd', p.astype(v.dtype), v)


def swa_prompt(x, w_qkv, q_g, k_g, sinks, w_out, w_buf):
    n, t, _ = x.shape
    nb = t // BLOCK
    q, k, v = qkv_heads(x, w_qkv, q_g, k_g)
    qb = q.reshape(n, nb, BLOCK, N_KV, GROUP, HEAD_DIM)

    def with_prev(z):
        zb = z.reshape(n, nb, BLOCK, N_KV, HEAD_DIM)
        prev = jnp.pad(zb, ((0, 0), (1, 0), (0, 0), (0, 0), (0, 0)))[:, :-1]
        return jnp.concatenate([prev, zb], axis=2)

    qi = jnp.arange(BLOCK)[:, None]
    si = jnp.arange(2 * BLOCK)[None, :]
    dist = qi + BLOCK - si
    key_pos = jnp.arange(nb)[:, None, None] * BLOCK + si[None] - BLOCK
    mask = (dist >= 0) & (dist < WINDOW) & (key_pos >= 0)
    o = sink_attention(qb, with_prev(k), with_prev(v), dist, mask[:, None, None], sinks)
    y = o.reshape(n, t, Q_DIM) @ w_out
    return y, k[:, t - w_buf:], v[:, t - w_buf:]


def swa_sample(x, k_cache, v_cache, w_qkv, q_g, k_g, sinks, w_out):
    n, s, _ = x.shape
    w_buf = k_cache.shape[1]
    q, k, v = qkv_heads(x, w_qkv, q_g, k_g)
    k_all = jnp.concatenate([k_cache.astype(k.dtype), k], axis=1)
    v_all = jnp.concatenate([v_cache.astype(v.dtype), v], axis=1)
    q_pos = PAST_LEN + jnp.arange(s)
    k_pos = jnp.concatenate([PAST_LEN - w_buf + jnp.arange(w_buf), q_pos])
    dist = q_pos[:, None] - k_pos[None, :]
    mask = (dist >= 0) & (dist < WINDOW)
    o = sink_attention(q, k_all, v_all, dist, mask, sinks)
    y = o.reshape(n, s, Q_DIM) @ w_out
    return y, k_all[:, s:], v_all[:, s:]


def sq_relu_mlp(x, w_up, w_down):
    return jnp.square(jax.nn.relu(x @ w_up)) @ w_down


def setup_inputs(seed: int = 0) -> dict:
    key = jax.random.key(seed)
    ks = jax.random.split(key, 24)
    f32 = jnp.float32
    w_buf = min(WINDOW, PAST_LEN)

    def nrm(k, shape, scale):
        return jax.random.normal(k, shape, f32) * scale

    u = jax.random.uniform(ks[15], (N_A, D_RNN), f32, minval=0.9, maxval=0.999)
    a_base = u ** (1.0 / LRU_C)
    lam = jnp.log(a_base) - jnp.log1p(-a_base)
    return {
        "x_prompt": nrm(ks[0], (BATCH, SEQ, D_MODEL), 1.0),
        "x_sample": nrm(ks[1], (DEC_BATCH, DEC_SEQ, D_MODEL), 1.0),
        "state_rglru_h": nrm(ks[2], (N_A, DEC_BATCH, D_RNN), 0.5),
        "state_rglru_conv": nrm(ks[3], (N_A, DEC_BATCH, CONV_W - 1, D_RNN), 1.0),
        "cache_swa_k": nrm(ks[4], (N_B, DEC_BATCH, w_buf, N_KV, HEAD_DIM), 1.0),
        "cache_swa_v": nrm(ks[5], (N_B, DEC_BATCH, w_buf, N_KV, HEAD_DIM), 1.0),
        "norm_mix_g": 1.0 + nrm(ks[6], (DEPTH, D_MODEL), 0.02),
        "norm_mlp_g": 1.0 + nrm(ks[7], (DEPTH, D_MODEL), 0.02),
        "lru_w_in": nrm(ks[8], (N_A, D_MODEL, 2 * D_RNN), D_MODEL ** -0.5),
        "lru_conv_w": nrm(ks[9], (N_A, CONV_W, D_RNN), CONV_W ** -0.5),
        "lru_conv_b": nrm(ks[10], (N_A, D_RNN), 0.01),
        "lru_w_a": nrm(ks[11], (N_A, LRU_BLOCKS, LRU_BW, LRU_BW), LRU_BW ** -0.5),
        "lru_b_a": nrm(ks[12], (N_A, D_RNN), 0.01),
        "lru_w_x": nrm(ks[13], (N_A, LRU_BLOCKS, LRU_BW, LRU_BW), LRU_BW ** -0.5),
        "lru_b_x": nrm(ks[14], (N_A, D_RNN), 0.01),
        "lru_lambda": lam,
        "lru_w_out": nrm(ks[16], (N_A, D_RNN, D_MODEL), D_RNN ** -0.5),
        "attn_w_qkv": nrm(ks[17], (N_B, D_MODEL, QKV_DIM), D_MODEL ** -0.5),
        "attn_q_norm": 1.0 + nrm(ks[18], (N_B, HEAD_DIM), 0.02),
        "attn_k_norm": 1.0 + nrm(ks[19], (N_B, HEAD_DIM), 0.02),
        "attn_sinks": nrm(ks[20], (N_B, N_HEADS), 0.5),
        "attn_w_out": nrm(ks[21], (N_B, Q_DIM, D_MODEL), Q_DIM ** -0.5),
        "mlp_w_up": nrm(ks[22], (DEPTH, D_MODEL, D_FF), D_MODEL ** -0.5),
        "mlp_w_down": nrm(ks[23], (DEPTH, D_FF, D_MODEL), 0.7 * D_FF ** -0.5),
    }


def reference(x_prompt, x_sample, state_rglru_h, state_rglru_conv, cache_swa_k, cache_swa_v,
              norm_mix_g, norm_mlp_g,
              lru_w_in, lru_conv_w, lru_conv_b, lru_w_a, lru_b_a, lru_w_x, lru_b_x, lru_lambda, lru_w_out,
              attn_w_qkv, attn_q_norm, attn_k_norm, attn_sinks, attn_w_out,
              mlp_w_up, mlp_w_down):
    n_p = x_prompt.shape[0]
    w_buf = cache_swa_k.shape[2]
    yp, ys = x_prompt, x_sample
    h_p_list, c_p_list, k_p_list, v_p_list = [], [], [], []
    h_s_list, c_s_list, k_s_list, v_s_list = [], [], [], []
    for layer in range(DEPTH):
        j = layer // N_MIXERS
        hp = rms_norm(yp, norm_mix_g[layer])
        hs = rms_norm(ys, norm_mix_g[layer])
        if layer % N_MIXERS == 0:
            params = (lru_w_in[j], lru_conv_w[j], lru_conv_b[j], lru_w_a[j], lru_b_a[j],
                      lru_w_x[j], lru_b_x[j], lru_lambda[j], lru_w_out[j])
            h0 = jnp.zeros((n_p, D_RNN), state_rglru_h.dtype)
            c0 = jnp.zeros((n_p, CONV_W - 1, D_RNN), hp.dtype)
            mp, h_p, c_p = recurrent_block(hp, h0, c0, *params)
            ms, h_s, c_s = recurrent_block(hs, state_rglru_h[j], state_rglru_conv[j], *params)
            h_p_list.append(h_p); c_p_list.append(c_p)
            h_s_list.append(h_s); c_s_list.append(c_s)
        else:
            mp, k_p, v_p = swa_prompt(hp, attn_w_qkv[j], attn_q_norm[j], attn_k_norm[j],
                                      attn_sinks[j], attn_w_out[j], w_buf)
            ms, k_s, v_s = swa_sample(hs, cache_swa_k[j], cache_swa_v[j], attn_w_qkv[j], attn_q_norm[j],
                                      attn_k_norm[j], attn_sinks[j], attn_w_out[j])
            k_p_list.append(k_p); v_p_list.append(v_p)
            k_s_list.append(k_s); v_s_list.append(v_s)
        yp = yp + mp
        ys = ys + ms
        yp = yp + sq_relu_mlp(rms_norm(yp, norm_mlp_g[layer]), mlp_w_up[layer], mlp_w_down[layer])
        ys = ys + sq_relu_mlp(rms_norm(ys, norm_mlp_g[layer]), mlp_w_up[layer], mlp_w_down[layer])
    return (yp, ys,
            jnp.stack(h_p_list), jnp.stack(c_p_list), jnp.stack(k_p_list), jnp.stack(v_p_list),
            jnp.stack(h_s_list), jnp.stack(c_s_list), jnp.stack(k_s_list), jnp.stack(v_s_list))
```

```python
import functools

import jax
import jax.numpy as jnp
from jax import lax
from jax.experimental import pallas as pl
from jax.experimental.pallas import tpu as pltpu

F32 = jnp.float32
BF16 = jnp.bfloat16

D_MODEL = 1024
D_RNN = 1024
D_FF = 4096
DEPTH = 4
CONV_W = 4
LRU_BLOCKS = 4
LRU_BW = D_RNN // LRU_BLOCKS
LRU_C = 8.0
N_HEADS = 16
N_KV = 4
GROUP = N_HEADS // N_KV
HEAD_DIM = 64
Q_DIM = N_HEADS * HEAD_DIM
KV_DIM = N_KV * HEAD_DIM
QKV_DIM = Q_DIM + 2 * KV_DIM
WINDOW = 128
RMS_EPS = 1e-6
NEG_INF = -1e30

V7X_VMEM_LIMIT_BYTES = 56 * 1024 * 1024
SUBLANES = 8

MLP_TM = 512
MLP_FC = 1024
LRU_TM = 256
SWA_TM = 256
SWA_SAMPLE_BS = 8


def _params(n_axes):
    return pltpu.CompilerParams(
        dimension_semantics=("arbitrary",) * n_axes,
        vmem_limit_bytes=V7X_VMEM_LIMIT_BYTES,
    )


def _const_spec(shape):
    nd = len(shape)
    return pl.BlockSpec(shape, lambda *_: (0,) * nd, pipeline_mode=pl.Buffered(1))


def _rms(x, g):
    return x * lax.rsqrt(jnp.mean(x * x, axis=-1, keepdims=True) + RMS_EPS) * g


def _dot(a, b):
    return jnp.dot(a, b, preferred_element_type=F32)


def _dot_nt(a, b):
    return lax.dot_general(a, b, (((1,), (1,)), ((), ())), preferred_element_type=F32)


def _mlp_kernel(x_ref, g_ref, wu_ref, wd_ref, o_ref):
    x = x_ref[...]
    xb = _rms(x, g_ref[...]).astype(BF16)
    acc = x
    for c in range(D_FF // MLP_FC):
        h = _dot(xb, wu_ref[:, c * MLP_FC:(c + 1) * MLP_FC])
        h = jnp.square(jnp.maximum(h, 0.0)).astype(BF16)
        acc = acc + _dot(h, wd_ref[c * MLP_FC:(c + 1) * MLP_FC, :])
    o_ref[...] = acc


def _mlp(x, g, wu, wd, tm):
    rows = x.shape[0]
    return pl.pallas_call(
        _mlp_kernel,
        grid=(rows // tm,),
        in_specs=[
            pl.BlockSpec((tm, D_MODEL), lambda i: (i, 0)),
            _const_spec((1, D_MODEL)),
            _const_spec((D_MODEL, D_FF)),
            _const_spec((D_FF, D_MODEL)),
        ],
        out_specs=pl.BlockSpec((tm, D_MODEL), lambda i: (i, 0)),
        out_shape=jax.ShapeDtypeStruct((rows, D_MODEL), F32),
        compiler_params=_params(1),
        name="mlp",
    )(x, g, wu, wd)


def _softplus(x):
    return jnp.maximum(x, 0.0) + jnp.log1p(jnp.exp(-jnp.abs(x)))


def _lru_gates(xc_k, wax_k, ba_k, bx_k, sp_k):
    ga = _dot(xc_k.astype(BF16), wax_k)
    r = jax.nn.sigmoid(ga[:, :LRU_BW] + ba_k)
    i = jax.nn.sigmoid(ga[:, LRU_BW:] + bx_k)
    log_a = -LRU_C * r * sp_k
    a = jnp.exp(log_a)
    u = jnp.sqrt(-jnp.tanh(log_a) * (a * a + 1.0)) * (i * xc_k)
    return a, u


def _scan_rows(a, u):
    rows = a.shape[0]
    row = lax.broadcasted_iota(jnp.int32, a.shape, 0)
    d = 1
    while d < rows:
        keep = row >= d
        u = jnp.where(keep, u + a * pltpu.roll(u, d, 0), u)
        a = jnp.where(keep, a * pltpu.roll(a, d, 0), a)
        d *= 2
    return a, u


def _lru_prompt_kernel(x_ref, g_ref, win_ref, cw_ref, cb_ref, wax_ref, ba_ref, bx_ref, lam_ref, wout_ref,
                       o_ref, h_ref, c_ref, xbuf, hcar):
    tm = x_ref.shape[1]
    j = pl.program_id(1)

    @pl.when(j == 0)
    def _():
        xbuf[0:SUBLANES, :] = jnp.zeros((SUBLANES, D_RNN), F32)
        hcar[...] = jnp.zeros_like(hcar)

    x = x_ref[0]
    xb = _rms(x, g_ref[...]).astype(BF16)
    u_in = _dot(xb, win_ref[...])
    xr = u_in[:, D_RNN:]
    xbuf[SUBLANES:SUBLANES + tm, :] = xr
    cw = cw_ref[...]
    xc = xbuf[pl.ds(SUBLANES - 3, tm), :] * cw[0:1, :]
    xc = xc + xbuf[pl.ds(SUBLANES - 2, tm), :] * cw[1:2, :]
    xc = xc + xbuf[pl.ds(SUBLANES - 1, tm), :] * cw[2:3, :]
    xc = xc + xr * cw[3:4, :]
    xc = xc + cb_ref[...]
    c_ref[0] = xbuf[pl.ds(SUBLANES + tm - (CONV_W - 1), CONV_W - 1), :]
    xbuf[0:SUBLANES, :] = xbuf[tm:tm + SUBLANES, :]

    sp = _softplus(-lam_ref[...])
    acc = x
    for k in range(LRU_BLOCKS):
        cs = slice(k * LRU_BW, (k + 1) * LRU_BW)
        a, u = _lru_gates(xc[:, cs], wax_ref[k], ba_ref[:, cs], bx_ref[:, cs], sp[:, cs])
        a, u = _scan_rows(a, u)
        h = a * hcar[:, cs] + u
        hcar[:, cs] = h[tm - 1:tm, :]
        y = h * jax.nn.gelu(u_in[:, cs])
        acc = acc + _dot(y.astype(BF16), wout_ref[cs, :])
    o_ref[0] = acc
    h_ref[0] = hcar[...]


def _lru_prompt(x, g, win, cw, cb, wax, ba, bx, lam, wout):
    n, t, _ = x.shape
    tm = LRU_TM
    return pl.pallas_call(
        _lru_prompt_kernel,
        grid=(n, t // tm),
        in_specs=[
            pl.BlockSpec((1, tm, D_MODEL), lambda b, j: (b, j, 0)),
            _const_spec((1, D_MODEL)),
            _const_spec((D_MODEL, 2 * D_RNN)),
            _const_spec((CONV_W, D_RNN)),
            _const_spec((1, D_RNN)),
            _const_spec((LRU_BLOCKS, LRU_BW, 2 * LRU_BW)),
            _const_spec((1, D_RNN)),
            _const_spec((1, D_RNN)),
            _const_spec((1, D_RNN)),
            _const_spec((D_RNN, D_MODEL)),
        ],
        out_specs=[
            pl.BlockSpec((1, tm, D_MODEL), lambda b, j: (b, j, 0)),
            pl.BlockSpec((1, 1, D_RNN), lambda b, j: (b, 0, 0)),
            pl.BlockSpec((1, CONV_W - 1, D_RNN), lambda b, j: (b, 0, 0)),
        ],
        out_shape=[
            jax.ShapeDtypeStruct((n, t, D_MODEL), F32),
            jax.ShapeDtypeStruct((n, 1, D_RNN), F32),
            jax.ShapeDtypeStruct((n, CONV_W - 1, D_RNN), F32),
        ],
        scratch_shapes=[
            pltpu.VMEM((SUBLANES + tm, D_RNN), F32),
            pltpu.VMEM((1, D_RNN), F32),
        ],
        compiler_params=_params(2),
        name="lru_prompt",
    )(x, g, win, cw, cb, wax, ba, bx, lam, wout)


def _lru_sample_kernel(x_ref, h0_ref, c0_ref, g_ref, win_ref, cw_ref, cb_ref, wax_ref, ba_ref, bx_ref, lam_ref,
                       wout_ref, o_ref, h_ref, c_ref):
    x = x_ref[...]
    xb = _rms(x, g_ref[...]).astype(BF16)
    u_in = _dot(xb, win_ref[...])
    xr = u_in[:, D_RNN:]
    cw = cw_ref[...]
    xc = c0_ref[0] * cw[0:1, :]
    xc = xc + c0_ref[1] * cw[1:2, :]
    xc = xc + c0_ref[2] * cw[2:3, :]
    xc = xc + xr * cw[3:4, :]
    xc = xc + cb_ref[...]
    c_ref[0] = c0_ref[1]
    c_ref[1] = c0_ref[2]
    c_ref[2] = xr

    sp = _softplus(-lam_ref[...])
    acc = x
    for k in range(LRU_BLOCKS):
        cs = slice(k * LRU_BW, (k + 1) * LRU_BW)
        a, u = _lru_gates(xc[:, cs], wax_ref[k], ba_ref[:, cs], bx_ref[:, cs], sp[:, cs])
        h = a * h0_ref[:, cs] + u
        h_ref[:, cs] = h
        y = h * jax.nn.gelu(u_in[:, cs])
        acc = acc + _dot(y.astype(BF16), wout_ref[cs, :])
    o_ref[...] = acc


def _lru_sample(x, h0, c0, g, win, cw, cb, wax, ba, bx, lam, wout):
    b = x.shape[0]
    shapes = [(b, D_MODEL), (b, D_RNN), (CONV_W - 1, b, D_RNN), (1, D_MODEL), (D_MODEL, 2 * D_RNN),
              (CONV_W, D_RNN), (1, D_RNN), (LRU_BLOCKS, LRU_BW, 2 * LRU_BW), (1, D_RNN), (1, D_RNN), (1, D_RNN),
              (D_RNN, D_MODEL)]
    return pl.pallas_call(
        _lru_sample_kernel,
        grid=(1,),
        in_specs=[_const_spec(s) for s in shapes],
        out_specs=[
            pl.BlockSpec((b, D_MODEL), lambda i: (0, 0)),
            pl.BlockSpec((b, D_RNN), lambda i: (0, 0)),
            pl.BlockSpec((CONV_W - 1, b, D_RNN), lambda i: (0, 0, 0)),
        ],
        out_shape=[
            jax.ShapeDtypeStruct((b, D_MODEL), F32),
            jax.ShapeDtypeStruct((b, D_RNN), F32),
            jax.ShapeDtypeStruct((CONV_W - 1, b, D_RNN), F32),
        ],
        compiler_params=_params(1),
        name="lru_sample",
    )(x, h0, c0, g, win, cw, cb, wax, ba, bx, lam, wout)


def _head_norm(z, g):
    return z * lax.rsqrt(jnp.mean(z * z, axis=-1, keepdims=True) + RMS_EPS) * g


def _swa_prompt_kernel(sink_ref, slope_ref, x_ref, g_ref, wqkv_ref, qg_ref, kg_ref, wout_ref,
                       o_ref, k_ref, v_ref, kbuf, vbuf, obuf):
    tm = x_ref.shape[1]
    nblk = tm // WINDOW
    j = pl.program_id(1)

    @pl.when(j == 0)
    def _():
        kbuf[:, 0:WINDOW, :] = jnp.zeros((N_KV, WINDOW, HEAD_DIM), BF16)
        vbuf[:, 0:WINDOW, :] = jnp.zeros((N_KV, WINDOW, HEAD_DIM), BF16)

    x = x_ref[0]
    xb = _rms(x, g_ref[...]).astype(BF16)
    qkv = _dot(xb, wqkv_ref[...])
    for kv in range(N_KV):
        kh = _head_norm(qkv[:, Q_DIM + kv * HEAD_DIM:Q_DIM + (kv + 1) * HEAD_DIM], kg_ref[...])
        vh = qkv[:, Q_DIM + KV_DIM + kv * HEAD_DIM:Q_DIM + KV_DIM + (kv + 1) * HEAD_DIM]
        kbuf[kv, WINDOW:WINDOW + tm, :] = kh.astype(BF16)
        vbuf[kv, WINDOW:WINDOW + tm, :] = vh.astype(BF16)
        k_ref[0, :, kv * HEAD_DIM:(kv + 1) * HEAD_DIM] = kh[tm - WINDOW:, :]
        v_ref[0, :, kv * HEAD_DIM:(kv + 1) * HEAD_DIM] = vh[tm - WINDOW:, :]

    qi = lax.broadcasted_iota(jnp.int32, (WINDOW, 2 * WINDOW), 0)
    si = lax.broadcasted_iota(jnp.int32, (WINDOW, 2 * WINDOW), 1)
    dist = qi + WINDOW - si
    dist_f = dist.astype(F32)
    mask = (dist >= 0) & (dist < WINDOW)
    mask0 = mask & ((si >= WINDOW) | (j > 0))

    for h in range(N_HEADS):
        kv = h // GROUP
        qh = _head_norm(qkv[:, h * HEAD_DIM:(h + 1) * HEAD_DIM], qg_ref[...]).astype(BF16)
        sink = sink_ref[h]
        slope = slope_ref[h]
        for i in range(nblk):
            keys = kbuf[kv, i * WINDOW:(i + 2) * WINDOW, :]
            vals = vbuf[kv, i * WINDOW:(i + 2) * WINDOW, :]
            s = _dot_nt(qh[i * WINDOW:(i + 1) * WINDOW, :], keys) * (HEAD_DIM ** -0.5)
            s = s - slope * dist_f
            s = jnp.where(mask0 if i == 0 else mask, s, NEG_INF)
            m = jnp.maximum(jnp.max(s, axis=-1, keepdims=True), sink)
            e = jnp.exp(s - m)
            p = e / (jnp.sum(e, axis=-1, keepdims=True) + jnp.exp(sink - m))
            obuf[i * WINDOW:(i + 1) * WINDOW, h * HEAD_DIM:(h + 1) * HEAD_DIM] = _dot(p.astype(BF16), vals).astype(BF16)

    kbuf[:, 0:WINDOW, :] = kbuf[:, tm:tm + WINDOW, :]
    vbuf[:, 0:WINDOW, :] = vbuf[:, tm:tm + WINDOW, :]
    o_ref[0] = x + _dot(obuf[...], wout_ref[...])


def _swa_prompt(x, sinks, slopes, g, wqkv, qg, kg, wout):
    n, t, _ = x.shape
    tm = SWA_TM
    smem = pl.BlockSpec(memory_space=pltpu.SMEM)
    return pl.pallas_call(
        _swa_prompt_kernel,
        grid=(n, t // tm),
        in_specs=[
            smem,
            smem,
            pl.BlockSpec((1, tm, D_MODEL), lambda b, j: (b, j, 0)),
            _const_spec((1, D_MODEL)),
            _const_spec((D_MODEL, QKV_DIM)),
            _const_spec((1, HEAD_DIM)),
            _const_spec((1, HEAD_DIM)),
            _const_spec((Q_DIM, D_MODEL)),
        ],
        out_specs=[
            pl.BlockSpec((1, tm, D_MODEL), lambda b, j: (b, j, 0)),
            pl.BlockSpec((1, WINDOW, KV_DIM), lambda b, j: (b, 0, 0)),
            pl.BlockSpec((1, WINDOW, KV_DIM), lambda b, j: (b, 0, 0)),
        ],
        out_shape=[
            jax.ShapeDtypeStruct((n, t, D_MODEL), F32),
            jax.ShapeDtypeStruct((n, WINDOW, KV_DIM), F32),
            jax.ShapeDtypeStruct((n, WINDOW, KV_DIM), F32),
        ],
        scratch_shapes=[
            pltpu.VMEM((N_KV, WINDOW + tm, HEAD_DIM), BF16),
            pltpu.VMEM((N_KV, WINDOW + tm, HEAD_DIM), BF16),
            pltpu.VMEM((tm, Q_DIM), BF16),
        ],
        compiler_params=_params(2),
        name="swa_prompt",
    )(sinks, slopes, x, g, wqkv, qg, kg, wout)


def _swa_sample_qkv_kernel(x_ref, g_ref, wqkv_ref, qg_ref, kg_ref, q_ref, k_ref, v_ref):
    xb = _rms(x_ref[...], g_ref[...]).astype(BF16)
    qkv = _dot(xb, wqkv_ref[...])
    for h in range(N_HEADS):
        cs = slice(h * HEAD_DIM, (h + 1) * HEAD_DIM)
        q_ref[:, cs] = _head_norm(qkv[:, cs], qg_ref[...])
    for kv in range(N_KV):
        cs = slice(kv * HEAD_DIM, (kv + 1) * HEAD_DIM)
        k_ref[:, cs] = _head_norm(qkv[:, Q_DIM + kv * HEAD_DIM:Q_DIM + (kv + 1) * HEAD_DIM], kg_ref[...])
    v_ref[...] = qkv[:, Q_DIM + KV_DIM:]


def _swa_sample_qkv(x, g, wqkv, qg, kg):
    b = x.shape[0]
    shapes = [(b, D_MODEL), (1, D_MODEL), (D_MODEL, QKV_DIM), (1, HEAD_DIM), (1, HEAD_DIM)]
    return pl.pallas_call(
        _swa_sample_qkv_kernel,
        grid=(1,),
        in_specs=[_const_spec(s) for s in shapes],
        out_specs=[
            pl.BlockSpec((b, Q_DIM), lambda i: (0, 0)),
            pl.BlockSpec((b, KV_DIM), lambda i: (0, 0)),
            pl.BlockSpec((b, KV_DIM), lambda i: (0, 0)),
        ],
        out_shape=[
            jax.ShapeDtypeStruct((b, Q_DIM), F32),
            jax.ShapeDtypeStruct((b, KV_DIM), F32),
            jax.ShapeDtypeStruct((b, KV_DIM), F32),
        ],
        compiler_params=_params(1),
        name="swa_sample_qkv",
    )(x, g, wqkv, qg, kg)


def _swa_sample_attn_kernel(q_ref, kn_ref, vn_ref, kc_ref, vc_ref, sink_ref, slope_ref,
                            o_ref, ko_ref, vo_ref):
    bs = q_ref.shape[0]
    w = kc_ref.shape[1]
    hrow = lax.broadcasted_iota(jnp.int32, (N_HEADS, KV_DIM), 0)
    lane = lax.broadcasted_iota(jnp.int32, (N_HEADS, KV_DIM), 1)
    own = (hrow // GROUP) == (lane // HEAD_DIM)
    key = lax.broadcasted_iota(jnp.int32, (N_HEADS, w), 1)
    dist = (w - key).astype(F32)
    valid = (w - key) < WINDOW
    sink = sink_ref[:, 0:1]
    slope = slope_ref[:, 0:1]

    def body(b, carry):
        kc = kc_ref[b]
        vc = vc_ref[b]
        kn = kn_ref[b]
        vn = vn_ref[b]
        q = q_ref[b]
        qt = jnp.concatenate([q] * N_KV, axis=1)
        qm = jnp.where(own, qt, 0.0).astype(BF16)
        s = _dot_nt(qm, kc.astype(BF16)) * (HEAD_DIM ** -0.5) - slope * dist
        s = jnp.where(valid, s, NEG_INF)
        s_new = jnp.sum(qm.astype(F32) * kn.astype(BF16).astype(F32), axis=-1, keepdims=True) * (HEAD_DIM ** -0.5)
        m = jnp.maximum(jnp.maximum(jnp.max(s, axis=-1, keepdims=True), s_new), sink)
        e = jnp.exp(s - m)
        e_new = jnp.exp(s_new - m)
        den = jnp.sum(e, axis=-1, keepdims=True) + e_new + jnp.exp(sink - m)
        p = (e / den).astype(BF16)
        p_new = (e_new / den).astype(BF16).astype(F32)
        res = _dot(p, vc.astype(BF16)) + p_new * vn.astype(BF16).astype(F32)
        res = jnp.where(own, res, 0.0)
        o = res[:, 0:HEAD_DIM]
        for kv in range(1, N_KV):
            o = o + res[:, kv * HEAD_DIM:(kv + 1) * HEAD_DIM]
        o_ref[b] = o
        ko_ref[b, pl.ds(0, w - 1), :] = kc_ref[b, pl.ds(1, w - 1), :]
        ko_ref[b, pl.ds(w - 1, 1), :] = kn
        vo_ref[b, pl.ds(0, w - 1), :] = vc_ref[b, pl.ds(1, w - 1), :]
        vo_ref[b, pl.ds(w - 1, 1), :] = vn
        return carry

    lax.fori_loop(0, bs, body, 0)


def _swa_sample_attn(q3, kn3, vn3, kc, vc, sink_b, slope_b):
    b, w, _ = kc.shape
    bs = SWA_SAMPLE_BS
    return pl.pallas_call(
        _swa_sample_attn_kernel,
        grid=(b // bs,),
        in_specs=[
            pl.BlockSpec((bs, N_HEADS, HEAD_DIM), lambda i: (i, 0, 0)),
            pl.BlockSpec((bs, 1, KV_DIM), lambda i: (i, 0, 0)),
            pl.BlockSpec((bs, 1, KV_DIM), lambda i: (i, 0, 0)),
            pl.BlockSpec((bs, w, KV_DIM), lambda i: (i, 0, 0)),
            pl.BlockSpec((bs, w, KV_DIM), lambda i: (i, 0, 0)),
            _const_spec((N_HEADS, 128)),
            _const_spec((N_HEADS, 128)),
        ],
        out_specs=[
            pl.BlockSpec((bs, N_HEADS, HEAD_DIM), lambda i: (i, 0, 0)),
            pl.BlockSpec((bs, w, KV_DIM), lambda i: (i, 0, 0)),
            pl.BlockSpec((bs, w, KV_DIM), lambda i: (i, 0, 0)),
        ],
        out_shape=[
            jax.ShapeDtypeStruct((b, N_HEADS, HEAD_DIM), F32),
            jax.ShapeDtypeStruct((b, w, KV_DIM), F32),
            jax.ShapeDtypeStruct((b, w, KV_DIM), F32),
        ],
        compiler_params=_params(1),
        name="swa_sample_attn",
    )(q3, kn3, vn3, kc, vc, sink_b, slope_b)


def _proj_residual_kernel(x_ref, o_ref_in, wout_ref, y_ref):
    y_ref[...] = x_ref[...] + _dot(o_ref_in[...].astype(BF16), wout_ref[...])


def _proj_residual(x, o, wout):
    b = x.shape[0]
    shapes = [(b, D_MODEL), (b, Q_DIM), (Q_DIM, D_MODEL)]
    return pl.pallas_call(
        _proj_residual_kernel,
        grid=(1,),
        in_specs=[_const_spec(s) for s in shapes],
        out_specs=pl.BlockSpec((b, D_MODEL), lambda i: (0, 0)),
        out_shape=jax.ShapeDtypeStruct((b, D_MODEL), F32),
        compiler_params=_params(1),
        name="proj_residual",
    )(x, o, wout)


def kernel(x_prompt, x_sample, state_rglru_h, state_rglru_conv, cache_swa_k, cache_swa_v, norm_mix_g, norm_mlp_g, lru_w_in, lru_conv_w, lru_conv_b, lru_w_a, lru_b_a, lru_w_x, lru_b_x, lru_lambda, lru_w_out, attn_w_qkv, attn_q_norm, attn_k_norm, attn_sinks, attn_w_out, mlp_w_up, mlp_w_down):
    n_p, t_p, _ = x_prompt.shape
    n_s = x_sample.shape[0]
    w_buf = cache_swa_k.shape[2]
    heads = jnp.arange(1, N_HEADS + 1, dtype=F32)
    slopes = jnp.exp2(-8.0 * heads / N_HEADS)
    slopes_b = jnp.broadcast_to(slopes[:, None], (N_HEADS, 128))

    row = lambda v: v.reshape(1, -1)
    yp = x_prompt
    ys = x_sample.reshape(n_s, D_MODEL)
    h_p, c_p, k_p, v_p, h_s, c_s, k_s, v_s = ([] for _ in range(8))
    for layer in range(DEPTH):
        j = layer // 2
        g_mix = row(norm_mix_g[layer])
        if layer % 2 == 0:
            win = lru_w_in[j].astype(BF16)
            wax = jnp.concatenate([lru_w_a[j], lru_w_x[j]], axis=-1).astype(BF16)
            args = (g_mix, win, lru_conv_w[j], row(lru_conv_b[j]), wax, row(lru_b_a[j]), row(lru_b_x[j]),
                    row(lru_lambda[j]), lru_w_out[j].astype(BF16))
            yp, hp, cp = _lru_prompt(yp, *args)
            ys, hs, cs = _lru_sample(ys, state_rglru_h[j], state_rglru_conv[j].transpose(1, 0, 2), *args)
            h_p.append(hp.reshape(n_p, D_RNN)); c_p.append(cp)
            h_s.append(hs); c_s.append(cs.transpose(1, 0, 2))
        else:
            wqkv = attn_w_qkv[j].astype(BF16)
            wout = attn_w_out[j].astype(BF16)
            qg, kg = row(attn_q_norm[j]), row(attn_k_norm[j])
            sinks = attn_sinks[j]
            yp, kp, vp = _swa_prompt(yp, sinks, slopes, g_mix, wqkv, qg, kg, wout)
            k_p.append(kp.reshape(n_p, w_buf, N_KV, HEAD_DIM)); v_p.append(vp.reshape(n_p, w_buf, N_KV, HEAD_DIM))
            q, kn, vn = _swa_sample_qkv(ys, g_mix, wqkv, qg, kg)
            o3, ks, vs = _swa_sample_attn(
                q.reshape(n_s, N_HEADS, HEAD_DIM), kn.reshape(n_s, 1, KV_DIM), vn.reshape(n_s, 1, KV_DIM),
                cache_swa_k[j].reshape(n_s, w_buf, KV_DIM), cache_swa_v[j].reshape(n_s, w_buf, KV_DIM),
                jnp.broadcast_to(sinks[:, None], (N_HEADS, 128)), slopes_b)
            ys = _proj_residual(ys, o3.reshape(n_s, Q_DIM), wout)
            k_s.append(ks.reshape(n_s, w_buf, N_KV, HEAD_DIM)); v_s.append(vs.reshape(n_s, w_buf, N_KV, HEAD_DIM))
        g_mlp = row(norm_mlp_g[layer])
        wu = mlp_w_up[layer].astype(BF16)
        wd = mlp_w_down[layer].astype(BF16)
        yp = _mlp(yp.reshape(n_p * t_p, D_MODEL), g_mlp, wu, wd, MLP_TM).reshape(n_p, t_p, D_MODEL)
        ys = _mlp(ys, g_mlp, wu, wd, n_s)
    return (yp, ys.reshape(n_s, 1, D_MODEL),
            jnp.stack(h_p), jnp.stack(c_p), jnp.stack(k_p), jnp.stack(v_p),
            jnp.stack(h_s), jnp.stack(c_s), jnp.stack(k_s), jnp.stack(v_s))
```

```python
import functools

import jax
import jax.numpy as jnp
from jax import lax
from jax.experimental import pallas as pl
from jax.experimental.pallas import tpu as pltpu

F32 = jnp.float32
BF16 = jnp.bfloat16

D_MODEL = 1024
D_RNN = 1024
D_FF = 4096
DEPTH = 4
CONV_W = 4
LRU_BLOCKS = 4
LRU_BW = D_RNN // LRU_BLOCKS
LRU_C = 8.0
N_HEADS = 16
N_KV = 4
GROUP = N_HEADS // N_KV
HEAD_DIM = 64
Q_DIM = N_HEADS * HEAD_DIM
KV_DIM = N_KV * HEAD_DIM
QKV_DIM = Q_DIM + 2 * KV_DIM
WINDOW = 128
assert GROUP * HEAD_DIM == KV_DIM and N_KV % 2 == 0 and GROUP % 2 == 0
RMS_EPS = 1e-6
NEG_INF = -1e30

V7X_VMEM_LIMIT_BYTES = 56 * 1024 * 1024
SUBLANES = 8

MLP_TM = 512
MLP_FC = 1024
LRU_TM = 256
SWA_TM = 256
SWA_SAMPLE_BS = 8


def _params(n_axes):
    return pltpu.CompilerParams(
        dimension_semantics=("arbitrary",) * n_axes,
        vmem_limit_bytes=V7X_VMEM_LIMIT_BYTES,
    )


def _const_spec(shape):
    nd = len(shape)
    return pl.BlockSpec(shape, lambda *_: (0,) * nd, pipeline_mode=pl.Buffered(1))


def _rms(x, g):
    return x * lax.rsqrt(jnp.mean(x * x, axis=-1, keepdims=True) + RMS_EPS) * g


def _dot(a, b):
    return jnp.dot(a, b, preferred_element_type=F32)


def _dot_nt(a, b):
    return lax.dot_general(a, b, (((1,), (1,)), ((), ())), preferred_element_type=F32)


def _mlp_kernel(x_ref, g_ref, wu_ref, wd_ref, o_ref):
    x = x_ref[...]
    xb = _rms(x, g_ref[...]).astype(BF16)
    acc = x
    for c in range(D_FF // MLP_FC):
        h = _dot(xb, wu_ref[:, c * MLP_FC:(c + 1) * MLP_FC])
        h = jnp.square(jnp.maximum(h, 0.0)).astype(BF16)
        acc = acc + _dot(h, wd_ref[c * MLP_FC:(c + 1) * MLP_FC, :])
    o_ref[...] = acc


def _mlp(x, g, wu, wd, tm):
    rows = x.shape[0]
    return pl.pallas_call(
        _mlp_kernel,
        grid=(rows // tm,),
        in_specs=[
            pl.BlockSpec((tm, D_MODEL), lambda i: (i, 0)),
            _const_spec((1, D_MODEL)),
            _const_spec((D_MODEL, D_FF)),
            _const_spec((D_FF, D_MODEL)),
        ],
        out_specs=pl.BlockSpec((tm, D_MODEL), lambda i: (i, 0)),
        out_shape=jax.ShapeDtypeStruct((rows, D_MODEL), F32),
        compiler_params=_params(1),
        name="mlp",
    )(x, g, wu, wd)


def _softplus(x):
    return jnp.maximum(x, 0.0) + jnp.log1p(jnp.exp(-jnp.abs(x)))


def _lru_gates(xc_k, wax_k, ba_k, bx_k, sp_k):
    ga = _dot(xc_k.astype(BF16), wax_k)
    r = jax.nn.sigmoid(ga[:, :LRU_BW] + ba_k)
    i = jax.nn.sigmoid(ga[:, LRU_BW:] + bx_k)
    log_a = -LRU_C * r * sp_k
    a = jnp.exp(log_a)
    u = jnp.sqrt(-jnp.tanh(log_a) * (a * a + 1.0)) * (i * xc_k)
    return a, u


def _scan_rows(a, u):
    rows = a.shape[0]
    row = lax.broadcasted_iota(jnp.int32, a.shape, 0)
    d = 1
    while d < rows:
        keep = row >= d
        u = jnp.where(keep, u + a * pltpu.roll(u, d, 0), u)
        a = jnp.where(keep, a * pltpu.roll(a, d, 0), a)
        d *= 2
    return a, u


def _lru_prompt_kernel(x_ref, g_ref, win_ref, cw_ref, cb_ref, wax_ref, ba_ref, bx_ref, lam_ref, wout_ref,
                       o_ref, h_ref, c_ref, xbuf, hcar):
    tm = x_ref.shape[1]
    j = pl.program_id(1)

    @pl.when(j == 0)
    def _():
        xbuf[0:SUBLANES, :] = jnp.zeros((SUBLANES, D_RNN), F32)
        hcar[...] = jnp.zeros_like(hcar)

    x = x_ref[0]
    xb = _rms(x, g_ref[...]).astype(BF16)
    u_in = _dot(xb, win_ref[...])
    xr = u_in[:, D_RNN:]
    xbuf[SUBLANES:SUBLANES + tm, :] = xr
    cw = cw_ref[...]
    xc = xbuf[pl.ds(SUBLANES - 3, tm), :] * cw[0:1, :]
    xc = xc + xbuf[pl.ds(SUBLANES - 2, tm), :] * cw[1:2, :]
    xc = xc + xbuf[pl.ds(SUBLANES - 1, tm), :] * cw[2:3, :]
    xc = xc + xr * cw[3:4, :]
    xc = xc + cb_ref[...]
    c_ref[0] = xbuf[pl.ds(SUBLANES + tm - (CONV_W - 1), CONV_W - 1), :]
    xbuf[0:SUBLANES, :] = xbuf[tm:tm + SUBLANES, :]

    sp = _softplus(-lam_ref[...])
    acc = x
    for k in range(LRU_BLOCKS):
        cs = slice(k * LRU_BW, (k + 1) * LRU_BW)
        a, u = _lru_gates(xc[:, cs], wax_ref[k], ba_ref[:, cs], bx_ref[:, cs], sp[:, cs])
        a, u = _scan_rows(a, u)
        h = a * hcar[:, cs] + u
        hcar[:, cs] = h[tm - 1:tm, :]
        y = h * jax.nn.gelu(u_in[:, cs])
        acc = acc + _dot(y.astype(BF16), wout_ref[cs, :])
    o_ref[0] = acc
    h_ref[0] = hcar[...]


def _lru_prompt(x, g, win, cw, cb, wax, ba, bx, lam, wout):
    n, t, _ = x.shape
    tm = LRU_TM
    return pl.pallas_call(
        _lru_prompt_kernel,
        grid=(n, t // tm),
        in_specs=[
            pl.BlockSpec((1, tm, D_MODEL), lambda b, j: (b, j, 0)),
            _const_spec((1, D_MODEL)),
            _const_spec((D_MODEL, 2 * D_RNN)),
            _const_spec((CONV_W, D_RNN)),
            _const_spec((1, D_RNN)),
            _const_spec((LRU_BLOCKS, LRU_BW, 2 * LRU_BW)),
            _const_spec((1, D_RNN)),
            _const_spec((1, D_RNN)),
            _const_spec((1, D_RNN)),
            _const_spec((D_RNN, D_MODEL)),
        ],
        out_specs=[
            pl.BlockSpec((1, tm, D_MODEL), lambda b, j: (b, j, 0)),
            pl.BlockSpec((1, 1, D_RNN), lambda b, j: (b, 0, 0)),
            pl.BlockSpec((1, CONV_W - 1, D_RNN), lambda b, j: (b, 0, 0)),
        ],
        out_shape=[
            jax.ShapeDtypeStruct((n, t, D_MODEL), F32),
            jax.ShapeDtypeStruct((n, 1, D_RNN), F32),
            jax.ShapeDtypeStruct((n, CONV_W - 1, D_RNN), F32),
        ],
        scratch_shapes=[
            pltpu.VMEM((SUBLANES + tm, D_RNN), F32),
            pltpu.VMEM((1, D_RNN), F32),
        ],
        compiler_params=_params(2),
        name="lru_prompt",
    )(x, g, win, cw, cb, wax, ba, bx, lam, wout)


def _lru_sample_kernel(x_ref, h0_ref, c0_ref, g_ref, win_ref, cw_ref, cb_ref, wax_ref, ba_ref, bx_ref, lam_ref,
                       wout_ref, o_ref, h_ref, c_ref):
    x = x_ref[...]
    xb = _rms(x, g_ref[...]).astype(BF16)
    u_in = _dot(xb, win_ref[...])
    xr = u_in[:, D_RNN:]
    cw = cw_ref[...]
    xc = c0_ref[0] * cw[0:1, :]
    xc = xc + c0_ref[1] * cw[1:2, :]
    xc = xc + c0_ref[2] * cw[2:3, :]
    xc = xc + xr * cw[3:4, :]
    xc = xc + cb_ref[...]
    c_ref[0] = c0_ref[1]
    c_ref[1] = c0_ref[2]
    c_ref[2] = xr

    sp = _softplus(-lam_ref[...])
    acc = x
    for k in range(LRU_BLOCKS):
        cs = slice(k * LRU_BW, (k + 1) * LRU_BW)
        a, u = _lru_gates(xc[:, cs], wax_ref[k], ba_ref[:, cs], bx_ref[:, cs], sp[:, cs])
        h = a * h0_ref[:, cs] + u
        h_ref[:, cs] = h
        y = h * jax.nn.gelu(u_in[:, cs])
        acc = acc + _dot(y.astype(BF16), wout_ref[cs, :])
    o_ref[...] = acc


def _lru_sample(x, h0, c0, g, win, cw, cb, wax, ba, bx, lam, wout):
    b = x.shape[0]
    shapes = [(b, D_MODEL), (b, D_RNN), (CONV_W - 1, b, D_RNN), (1, D_MODEL), (D_MODEL, 2 * D_RNN),
              (CONV_W, D_RNN), (1, D_RNN), (LRU_BLOCKS, LRU_BW, 2 * LRU_BW), (1, D_RNN), (1, D_RNN), (1, D_RNN),
              (D_RNN, D_MODEL)]
    return pl.pallas_call(
        _lru_sample_kernel,
        grid=(1,),
        in_specs=[_const_spec(s) for s in shapes],
        out_specs=[
            pl.BlockSpec((b, D_MODEL), lambda i: (0, 0)),
            pl.BlockSpec((b, D_RNN), lambda i: (0, 0)),
            pl.BlockSpec((CONV_W - 1, b, D_RNN), lambda i: (0, 0, 0)),
        ],
        out_shape=[
            jax.ShapeDtypeStruct((b, D_MODEL), F32),
            jax.ShapeDtypeStruct((b, D_RNN), F32),
            jax.ShapeDtypeStruct((CONV_W - 1, b, D_RNN), F32),
        ],
        compiler_params=_params(1),
        name="lru_sample",
    )(x, h0, c0, g, win, cw, cb, wax, ba, bx, lam, wout)


def _head_norm(z, g):
    return z * lax.rsqrt(jnp.mean(z * z, axis=-1, keepdims=True) + RMS_EPS) * g


def _heads_norm(z, ones, g):
    z2 = z * z
    hi = z2.astype(BF16)
    lo = (z2 - hi.astype(F32)).astype(BF16)
    ssq = _dot(hi, ones) + _dot(lo, ones)
    return z * lax.rsqrt(ssq * (1.0 / HEAD_DIM) + RMS_EPS) * g


def _swa_prompt_kernel(sink_ref, slope_ref, x_ref, g_ref, wqkv_ref, qg_ref, kg_ref, ones_ref, wout_ref,
                       o_ref, k_ref, v_ref, k2, v2, qn, obuf, bias):
    tm = x_ref.shape[1]
    nblk = tm // WINDOW
    b = pl.program_id(0)
    j = pl.program_id(1)

    @pl.when((b == 0) & (j == 0))
    def _():
        qi = lax.broadcasted_iota(jnp.int32, (WINDOW, 2 * WINDOW), 0)
        si = lax.broadcasted_iota(jnp.int32, (WINDOW, 2 * WINDOW), 1)
        dist = qi + WINDOW - si
        valid = (dist >= 0) & (dist < WINDOW)
        dist_f = dist.astype(F32)
        for h in range(N_HEADS):
            bias[h] = jnp.where(valid, -(slope_ref[h] * dist_f), NEG_INF)

    @pl.when(j == 0)
    def _():
        k2[:, :, 0:WINDOW, :] = jnp.zeros((N_KV, 2, WINDOW, 2 * HEAD_DIM), BF16)
        v2[:, :, 0:WINDOW, :] = jnp.zeros((N_KV, 2, WINDOW, 2 * HEAD_DIM), BF16)

    x = x_ref[0]
    xb = _rms(x, g_ref[...]).astype(BF16)
    qkv = _dot(xb, wqkv_ref[...])

    ones = ones_ref[...]
    kn = _heads_norm(qkv[:, Q_DIM:Q_DIM + KV_DIM], ones, kg_ref[...])
    vv = qkv[:, Q_DIM + KV_DIM:]
    k_ref[0] = kn[tm - WINDOW:, :]
    v_ref[0] = vv[tm - WINDOW:, :]
    low = lax.broadcasted_iota(jnp.int32, (tm, 2 * HEAD_DIM), 1) < HEAD_DIM
    for src, dst in ((kn, k2), (vv, v2)):
        for sl in range(N_KV // 2):
            z = src[:, sl * 2 * HEAD_DIM:(sl + 1) * 2 * HEAD_DIM]
            zs = pltpu.roll(z, HEAD_DIM, 1)
            dst[2 * sl, 0, WINDOW:WINDOW + tm, :] = jnp.where(low, z, 0.0).astype(BF16)
            dst[2 * sl, 1, WINDOW:WINDOW + tm, :] = jnp.where(low, 0.0, zs).astype(BF16)
            dst[2 * sl + 1, 0, WINDOW:WINDOW + tm, :] = jnp.where(low, zs, 0.0).astype(BF16)
            dst[2 * sl + 1, 1, WINDOW:WINDOW + tm, :] = jnp.where(low, 0.0, z).astype(BF16)
    qg = qg_ref[...] * (HEAD_DIM ** -0.5)
    for kv in range(N_KV):
        cs = slice(kv * KV_DIM, (kv + 1) * KV_DIM)
        qn[:, cs] = _heads_norm(qkv[:, cs], ones, qg).astype(BF16)

    col = lax.broadcasted_iota(jnp.int32, (1, 2 * WINDOW), 1)
    no_prev = jnp.where((col < WINDOW) & (j == 0), NEG_INF, 0.0)

    def scores(kv, i):
        rq = slice(i * WINDOW, (i + 1) * WINDOW)
        ql = jnp.concatenate([qn[rq, kv * KV_DIM + p * 2 * HEAD_DIM:kv * KV_DIM + (p + 1) * 2 * HEAD_DIM]
                              for p in range(GROUP // 2)], axis=0)
        return [_dot_nt(ql, k2[kv, c, i * WINDOW:(i + 2) * WINDOW, :]) for c in range(2)]

    work = [(kv, i) for kv in range(N_KV) for i in range(nblk)]
    s_next = scores(*work[0])
    for n, (kv, i) in enumerate(work):
        if True:
            rq = slice(i * WINDOW, (i + 1) * WINDOW)
            rk = slice(i * WINDOW, (i + 2) * WINDOW)
            s = s_next
            if n + 1 < len(work):
                s_next = scores(*work[n + 1])
            probs = [[None, None], [None, None]]
            for g in range(GROUP):
                p, c = divmod(g, 2)
                h = kv * GROUP + g
                sink = sink_ref[h]
                sg = s[c][p * WINDOW:(p + 1) * WINDOW, :] + bias[h]
                if i == 0:
                    sg = sg + no_prev
                m = jnp.maximum(jnp.max(sg, axis=-1, keepdims=True), sink)
                e = jnp.exp(sg - m)
                pr = e / (jnp.sum(e, axis=-1, keepdims=True) + jnp.exp(sink - m))
                probs[c][p] = pr.astype(BF16)
            pv = (_dot(jnp.concatenate(probs[0], axis=0), v2[kv, 0, rk, :])
                  + _dot(jnp.concatenate(probs[1], axis=0), v2[kv, 1, rk, :]))
            for p in range(GROUP // 2):
                obuf[rq, kv * KV_DIM + p * 2 * HEAD_DIM:kv * KV_DIM + (p + 1) * 2 * HEAD_DIM] = (
                    pv[p * WINDOW:(p + 1) * WINDOW, :].astype(BF16))

    k2[:, :, 0:WINDOW, :] = k2[:, :, tm:tm + WINDOW, :]
    v2[:, :, 0:WINDOW, :] = v2[:, :, tm:tm + WINDOW, :]
    o_ref[0] = x + _dot(obuf[...], wout_ref[...])


def _swa_prompt(x, sinks, slopes, g, wqkv, qg4, kg4, ones, wout):
    n, t, _ = x.shape
    tm = SWA_TM
    smem = pl.BlockSpec(memory_space=pltpu.SMEM)
    return pl.pallas_call(
        _swa_prompt_kernel,
        grid=(n, t // tm),
        in_specs=[
            smem,
            smem,
            pl.BlockSpec((1, tm, D_MODEL), lambda b, j: (b, j, 0)),
            _const_spec((1, D_MODEL)),
            _const_spec((D_MODEL, QKV_DIM)),
            _const_spec((1, KV_DIM)),
            _const_spec((1, KV_DIM)),
            _const_spec((KV_DIM, KV_DIM)),
            _const_spec((Q_DIM, D_MODEL)),
        ],
        out_specs=[
            pl.BlockSpec((1, tm, D_MODEL), lambda b, j: (b, j, 0)),
            pl.BlockSpec((1, WINDOW, KV_DIM), lambda b, j: (b, 0, 0)),
            pl.BlockSpec((1, WINDOW, KV_DIM), lambda b, j: (b, 0, 0)),
        ],
        out_shape=[
            jax.ShapeDtypeStruct((n, t, D_MODEL), F32),
            jax.ShapeDtypeStruct((n, WINDOW, KV_DIM), F32),
            jax.ShapeDtypeStruct((n, WINDOW, KV_DIM), F32),
        ],
        scratch_shapes=[
            pltpu.VMEM((N_KV, 2, WINDOW + tm, 2 * HEAD_DIM), BF16),
            pltpu.VMEM((N_KV, 2, WINDOW + tm, 2 * HEAD_DIM), BF16),
            pltpu.VMEM((tm, Q_DIM), BF16),
            pltpu.VMEM((tm, Q_DIM), BF16),
            pltpu.VMEM((N_HEADS, WINDOW, 2 * WINDOW), F32),
        ],
        compiler_params=_params(2),
        name="swa_prompt",
    )(sinks, slopes, x, g, wqkv, qg4, kg4, ones, wout)


def _swa_sample_qkv_kernel(x_ref, g_ref, wqkv_ref, qg_ref, kg_ref, q_ref, k_ref, v_ref):
    xb = _rms(x_ref[...], g_ref[...]).astype(BF16)
    qkv = _dot(xb, wqkv_ref[...])
    for h in range(N_HEADS):
        cs = slice(h * HEAD_DIM, (h + 1) * HEAD_DIM)
        q_ref[:, cs] = _head_norm(qkv[:, cs], qg_ref[...])
    for kv in range(N_KV):
        cs = slice(kv * HEAD_DIM, (kv + 1) * HEAD_DIM)
        k_ref[:, cs] = _head_norm(qkv[:, Q_DIM + kv * HEAD_DIM:Q_DIM + (kv + 1) * HEAD_DIM], kg_ref[...])
    v_ref[...] = qkv[:, Q_DIM + KV_DIM:]


def _swa_sample_qkv(x, g, wqkv, qg, kg):
    b = x.shape[0]
    shapes = [(b, D_MODEL), (1, D_MODEL), (D_MODEL, QKV_DIM), (1, HEAD_DIM), (1, HEAD_DIM)]
    return pl.pallas_call(
        _swa_sample_qkv_kernel,
        grid=(1,),
        in_specs=[_const_spec(s) for s in shapes],
        out_specs=[
            pl.BlockSpec((b, Q_DIM), lambda i: (0, 0)),
            pl.BlockSpec((b, KV_DIM), lambda i: (0, 0)),
            pl.BlockSpec((b, KV_DIM), lambda i: (0, 0)),
        ],
        out_shape=[
            jax.ShapeDtypeStruct((b, Q_DIM), F32),
            jax.ShapeDtypeStruct((b, KV_DIM), F32),
            jax.ShapeDtypeStruct((b, KV_DIM), F32),
        ],
        compiler_params=_params(1),
        name="swa_sample_qkv",
    )(x, g, wqkv, qg, kg)


def _swa_sample_attn_kernel(q_ref, kn_ref, vn_ref, kc_ref, vc_ref, sink_ref, slope_ref,
                            o_ref, ko_ref, vo_ref):
    bs = q_ref.shape[0]
    w = kc_ref.shape[1]
    hrow = lax.broadcasted_iota(jnp.int32, (N_HEADS, KV_DIM), 0)
    lane = lax.broadcasted_iota(jnp.int32, (N_HEADS, KV_DIM), 1)
    own = (hrow // GROUP) == (lane // HEAD_DIM)
    key = lax.broadcasted_iota(jnp.int32, (N_HEADS, w), 1)
    dist = (w - key).astype(F32)
    valid = (w - key) < WINDOW
    sink = sink_ref[:, 0:1]
    slope = slope_ref[:, 0:1]

    def body(b, carry):
        kc = kc_ref[b]
        vc = vc_ref[b]
        kn = kn_ref[b]
        vn = vn_ref[b]
        q = q_ref[b]
        qt = jnp.concatenate([q] * N_KV, axis=1)
        qm = jnp.where(own, qt, 0.0).astype(BF16)
        s = _dot_nt(qm, kc.astype(BF16)) * (HEAD_DIM ** -0.5) - slope * dist
        s = jnp.where(valid, s, NEG_INF)
        s_new = jnp.sum(qm.astype(F32) * kn.astype(BF16).astype(F32), axis=-1, keepdims=True) * (HEAD_DIM ** -0.5)
        m = jnp.maximum(jnp.maximum(jnp.max(s, axis=-1, keepdims=True), s_new), sink)
        e = jnp.exp(s - m)
        e_new = jnp.exp(s_new - m)
        den = jnp.sum(e, axis=-1, keepdims=True) + e_new + jnp.exp(sink - m)
        p = (e / den).astype(BF16)
        p_new = (e_new / den).astype(BF16).astype(F32)
        res = _dot(p, vc.astype(BF16)) + p_new * vn.astype(BF16).astype(F32)
        res = jnp.where(own, res, 0.0)
        o = res[:, 0:HEAD_DIM]
        for kv in range(1, N_KV):
            o = o + res[:, kv * HEAD_DIM:(kv + 1) * HEAD_DIM]
        o_ref[b] = o
        ko_ref[b, pl.ds(0, w - 1), :] = kc_ref[b, pl.ds(1, w - 1), :]
        ko_ref[b, pl.ds(w - 1, 1), :] = kn
        vo_ref[b, pl.ds(0, w - 1), :] = vc_ref[b, pl.ds(1, w - 1), :]
        vo_ref[b, pl.ds(w - 1, 1), :] = vn
        return carry

    lax.fori_loop(0, bs, body, 0)


def _swa_sample_attn(q3, kn3, vn3, kc, vc, sink_b, slope_b):
    b, w, _ = kc.shape
    bs = SWA_SAMPLE_BS
    return pl.pallas_call(
        _swa_sample_attn_kernel,
        grid=(b // bs,),
        in_specs=[
            pl.BlockSpec((bs, N_HEADS, HEAD_DIM), lambda i: (i, 0, 0)),
            pl.BlockSpec((bs, 1, KV_DIM), lambda i: (i, 0, 0)),
            pl.BlockSpec((bs, 1, KV_DIM), lambda i: (i, 0, 0)),
            pl.BlockSpec((bs, w, KV_DIM), lambda i: (i, 0, 0)),
            pl.BlockSpec((bs, w, KV_DIM), lambda i: (i, 0, 0)),
            _const_spec((N_HEADS, 128)),
            _const_spec((N_HEADS, 128)),
        ],
        out_specs=[
            pl.BlockSpec((bs, N_HEADS, HEAD_DIM), lambda i: (i, 0, 0)),
            pl.BlockSpec((bs, w, KV_DIM), lambda i: (i, 0, 0)),
            pl.BlockSpec((bs, w, KV_DIM), lambda i: (i, 0, 0)),
        ],
        out_shape=[
            jax.ShapeDtypeStruct((b, N_HEADS, HEAD_DIM), F32),
            jax.ShapeDtypeStruct((b, w, KV_DIM), F32),
            jax.ShapeDtypeStruct((b, w, KV_DIM), F32),
        ],
        compiler_params=_params(1),
        name="swa_sample_attn",
    )(q3, kn3, vn3, kc, vc, sink_b, slope_b)


def _proj_residual_kernel(x_ref, o_ref_in, wout_ref, y_ref):
    y_ref[...] = x_ref[...] + _dot(o_ref_in[...].astype(BF16), wout_ref[...])


def _proj_residual(x, o, wout):
    b = x.shape[0]
    shapes = [(b, D_MODEL), (b, Q_DIM), (Q_DIM, D_MODEL)]
    return pl.pallas_call(
        _proj_residual_kernel,
        grid=(1,),
        in_specs=[_const_spec(s) for s in shapes],
        out_specs=pl.BlockSpec((b, D_MODEL), lambda i: (0, 0)),
        out_shape=jax.ShapeDtypeStruct((b, D_MODEL), F32),
        compiler_params=_params(1),
        name="proj_residual",
    )(x, o, wout)


def kernel(x_prompt, x_sample, state_rglru_h, state_rglru_conv, cache_swa_k, cache_swa_v, norm_mix_g, norm_mlp_g, lru_w_in, lru_conv_w, lru_conv_b, lru_w_a, lru_b_a, lru_w_x, lru_b_x, lru_lambda, lru_w_out, attn_w_qkv, attn_q_norm, attn_k_norm, attn_sinks, attn_w_out, mlp_w_up, mlp_w_down):
    n_p, t_p, _ = x_prompt.shape
    n_s = x_sample.shape[0]
    w_buf = cache_swa_k.shape[2]
    heads = jnp.arange(1, N_HEADS + 1, dtype=F32)
    slopes = jnp.exp2(-8.0 * heads / N_HEADS)
    slopes_b = jnp.broadcast_to(slopes[:, None], (N_HEADS, 128))
    head_of_lane = jnp.arange(KV_DIM) // HEAD_DIM
    head_ones = (head_of_lane[:, None] == head_of_lane[None, :]).astype(BF16)

    row = lambda v: v.reshape(1, -1)
    yp = x_prompt
    ys = x_sample.reshape(n_s, D_MODEL)
    h_p, c_p, k_p, v_p, h_s, c_s, k_s, v_s = ([] for _ in range(8))
    for layer in range(DEPTH):
        j = layer // 2
        g_mix = row(norm_mix_g[layer])
        if layer % 2 == 0:
            win = lru_w_in[j].astype(BF16)
            wax = jnp.concatenate([lru_w_a[j], lru_w_x[j]], axis=-1).astype(BF16)
            args = (g_mix, win, lru_conv_w[j], row(lru_conv_b[j]), wax, row(lru_b_a[j]), row(lru_b_x[j]),
                    row(lru_lambda[j]), lru_w_out[j].astype(BF16))
            yp, hp, cp = _lru_prompt(yp, *args)
            ys, hs, cs = _lru_sample(ys, state_rglru_h[j], state_rglru_conv[j].transpose(1, 0, 2), *args)
            h_p.append(hp.reshape(n_p, D_RNN)); c_p.append(cp)
            h_s.append(hs); c_s.append(cs.transpose(1, 0, 2))
        else:
            wqkv = attn_w_qkv[j].astype(BF16)
            wout = attn_w_out[j].astype(BF16)
            qg, kg = row(attn_q_norm[j]), row(attn_k_norm[j])
            sinks = attn_sinks[j]
            yp, kp, vp = _swa_prompt(yp, sinks, slopes, g_mix, wqkv, jnp.tile(qg, (1, GROUP)),
                                     jnp.tile(kg, (1, N_KV)), head_ones, wout)
            k_p.append(kp.reshape(n_p, w_buf, N_KV, HEAD_DIM)); v_p.append(vp.reshape(n_p, w_buf, N_KV, HEAD_DIM))
            q, kn, vn = _swa_sample_qkv(ys, g_mix, wqkv, qg, kg)
            o3, ks, vs = _swa_sample_attn(
                q.reshape(n_s, N_HEADS, HEAD_DIM), kn.reshape(n_s, 1, KV_DIM), vn.reshape(n_s, 1, KV_DIM),
                cache_swa_k[j].reshape(n_s, w_buf, KV_DIM), cache_swa_v[j].reshape(n_s, w_buf, KV_DIM),
                jnp.broadcast_to(sinks[:, None], (N_HEADS, 128)), slopes_b)
            ys = _proj_residual(ys, o3.reshape(n_s, Q_DIM), wout)
            k_s.append(ks.reshape(n_s, w_buf, N_KV, HEAD_DIM)); v_s.append(vs.reshape(n_s, w_buf, N_KV, HEAD_DIM))
        g_mlp = row(norm_mlp_g[layer])
        wu = mlp_w_up[layer].astype(BF16)
        wd = mlp_w_down[layer].astype(BF16)
        yp = _mlp(yp.reshape(n_p * t_p, D_MODEL), g_mlp, wu, wd, MLP_TM).reshape(n_p, t_p, D_MODEL)
        ys = _mlp(ys, g_mlp, wu, wd, n_s)
    return (yp, ys.reshape(n_s, 1, D_MODEL),
            jnp.stack(h_p), jnp.stack(c_p), jnp.stack(k_p), jnp.stack(v_p),
            jnp.stack(h_s), jnp.stack(c_s), jnp.stack(k_s), jnp.stack(v_s))
```

```python
import functools

import jax
import jax.numpy as jnp
from jax import lax
from jax.experimental import pallas as pl
from jax.experimental.pallas import tpu as pltpu

F32 = jnp.float32
BF16 = jnp.bfloat16

D_MODEL = 1024
D_RNN = 1024
D_FF = 4096
DEPTH = 4
CONV_W = 4
LRU_BLOCKS = 4
LRU_BW = D_RNN // LRU_BLOCKS
LRU_C = 8.0
N_HEADS = 16
N_KV = 4
GROUP = N_HEADS // N_KV
HEAD_DIM = 64
Q_DIM = N_HEADS * HEAD_DIM
KV_DIM = N_KV * HEAD_DIM
QKV_DIM = Q_DIM + 2 * KV_DIM
WINDOW = 128
assert GROUP * HEAD_DIM == KV_DIM and N_KV % 2 == 0 and GROUP % 2 == 0
RMS_EPS = 1e-6
NEG_INF = -1e30

V7X_VMEM_LIMIT_BYTES = 56 * 1024 * 1024
SUBLANES = 8

MLP_TM = 512
MLP_FC = 1024
LRU_TM = 256
SWA_TM = 256
SWA_SAMPLE_BS = 16
SWA_SAMPLE_UNROLL = 8


def _params(n_axes):
    return pltpu.CompilerParams(
        dimension_semantics=("arbitrary",) * n_axes,
        vmem_limit_bytes=V7X_VMEM_LIMIT_BYTES,
    )


def _const_spec(shape):
    nd = len(shape)
    return pl.BlockSpec(shape, lambda *_: (0,) * nd, pipeline_mode=pl.Buffered(1))


def _rms(x, g):
    return x * lax.rsqrt(jnp.mean(x * x, axis=-1, keepdims=True) + RMS_EPS) * g


def _dot(a, b):
    return jnp.dot(a, b, preferred_element_type=F32)


def _dot_nt(a, b):
    return lax.dot_general(a, b, (((1,), (1,)), ((), ())), preferred_element_type=F32)


def _mlp_kernel(x_ref, g_ref, wu_ref, wd_ref, o_ref):
    x = x_ref[...]
    xb = _rms(x, g_ref[...]).astype(BF16)
    acc = x
    for c in range(D_FF // MLP_FC):
        h = _dot(xb, wu_ref[:, c * MLP_FC:(c + 1) * MLP_FC])
        h = jnp.square(jnp.maximum(h, 0.0)).astype(BF16)
        acc = acc + _dot(h, wd_ref[c * MLP_FC:(c + 1) * MLP_FC, :])
    o_ref[...] = acc


def _mlp(x, g, wu, wd, tm):
    rows = x.shape[0]
    return pl.pallas_call(
        _mlp_kernel,
        grid=(rows // tm,),
        in_specs=[
            pl.BlockSpec((tm, D_MODEL), lambda i: (i, 0)),
            _const_spec((1, D_MODEL)),
            _const_spec((D_MODEL, D_FF)),
            _const_spec((D_FF, D_MODEL)),
        ],
        out_specs=pl.BlockSpec((tm, D_MODEL), lambda i: (i, 0)),
        out_shape=jax.ShapeDtypeStruct((rows, D_MODEL), F32),
        compiler_params=_params(1),
        name="mlp",
    )(x, g, wu, wd)


def _softplus(x):
    return jnp.maximum(x, 0.0) + jnp.log1p(jnp.exp(-jnp.abs(x)))


def _lru_gates(xc_k, ga, ba_k, bx_k, sp_k):
    r = jax.nn.sigmoid(ga[:, :LRU_BW] + ba_k)
    i = jax.nn.sigmoid(ga[:, LRU_BW:] + bx_k)
    log_a = -LRU_C * r * sp_k
    a = jnp.exp(log_a)
    u = jnp.sqrt(-jnp.tanh(log_a) * (a * a + 1.0)) * (i * xc_k)
    return a, u


def _scan_rows(a, u, h0):
    rows, c = a.shape
    groups = rows // SUBLANES
    a = a.reshape(groups, SUBLANES, c)
    u = u.reshape(groups, SUBLANES, c)
    row = lax.broadcasted_iota(jnp.int32, a.shape, 1)
    d = 1
    while d < SUBLANES:
        keep = row >= d
        u = jnp.where(keep, u + a * pltpu.roll(u, d, 1), u)
        a = jnp.where(keep, a * pltpu.roll(a, d, 1), a)
        d *= 2
    hs = []
    h = h0
    for g in range(groups):
        hg = a[g] * h + u[g]
        hs.append(hg)
        h = hg[SUBLANES - 1:SUBLANES, :]
    return jnp.concatenate(hs, axis=0)


def _lru_prompt_kernel(x_ref, g_ref, win_ref, cw_ref, cb_ref, wax_ref, ba_ref, bx_ref, lam_ref, wout_ref,
                       o_ref, h_ref, c_ref, xbuf, hcar):
    tm = x_ref.shape[1]
    j = pl.program_id(1)

    @pl.when(j == 0)
    def _():
        xbuf[0:SUBLANES, :] = jnp.zeros((SUBLANES, D_RNN), F32)
        hcar[...] = jnp.zeros_like(hcar)

    x = x_ref[0]
    xb = _rms(x, g_ref[...]).astype(BF16)
    u_in = _dot(xb, win_ref[...])
    xr = u_in[:, D_RNN:]
    xbuf[SUBLANES:SUBLANES + tm, :] = xr
    cw = cw_ref[...]
    xc = xbuf[pl.ds(SUBLANES - 3, tm), :] * cw[0:1, :]
    xc = xc + xbuf[pl.ds(SUBLANES - 2, tm), :] * cw[1:2, :]
    xc = xc + xbuf[pl.ds(SUBLANES - 1, tm), :] * cw[2:3, :]
    xc = xc + xr * cw[3:4, :]
    xc = xc + cb_ref[...]
    c_ref[0] = xbuf[pl.ds(SUBLANES + tm - (CONV_W - 1), CONV_W - 1), :]
    xbuf[0:SUBLANES, :] = xbuf[tm:tm + SUBLANES, :]

    sp = _softplus(-lam_ref[...])
    blocks = [slice(k * LRU_BW, (k + 1) * LRU_BW) for k in range(LRU_BLOCKS)]
    gates = [_dot(xc[:, cs].astype(BF16), wax_ref[k]) for k, cs in enumerate(blocks)]
    acc = x
    for k, cs in enumerate(blocks):
        a, u = _lru_gates(xc[:, cs], gates[k], ba_ref[:, cs], bx_ref[:, cs], sp[:, cs])
        h = _scan_rows(a, u, hcar[:, cs])
        hcar[:, cs] = h[tm - 1:tm, :]
        y = h * jax.nn.gelu(u_in[:, cs])
        acc = acc + _dot(y.astype(BF16), wout_ref[cs, :])
    o_ref[0] = acc
    h_ref[0] = hcar[...]


def _lru_prompt(x, g, win, cw, cb, wax, ba, bx, lam, wout):
    n, t, _ = x.shape
    tm = LRU_TM
    return pl.pallas_call(
        _lru_prompt_kernel,
        grid=(n, t // tm),
        in_specs=[
            pl.BlockSpec((1, tm, D_MODEL), lambda b, j: (b, j, 0)),
            _const_spec((1, D_MODEL)),
            _const_spec((D_MODEL, 2 * D_RNN)),
            _const_spec((CONV_W, D_RNN)),
            _const_spec((1, D_RNN)),
            _const_spec((LRU_BLOCKS, LRU_BW, 2 * LRU_BW)),
            _const_spec((1, D_RNN)),
            _const_spec((1, D_RNN)),
            _const_spec((1, D_RNN)),
            _const_spec((D_RNN, D_MODEL)),
        ],
        out_specs=[
            pl.BlockSpec((1, tm, D_MODEL), lambda b, j: (b, j, 0)),
            pl.BlockSpec((1, 1, D_RNN), lambda b, j: (b, 0, 0)),
            pl.BlockSpec((1, CONV_W - 1, D_RNN), lambda b, j: (b, 0, 0)),
        ],
        out_shape=[
            jax.ShapeDtypeStruct((n, t, D_MODEL), F32),
            jax.ShapeDtypeStruct((n, 1, D_RNN), F32),
            jax.ShapeDtypeStruct((n, CONV_W - 1, D_RNN), F32),
        ],
        scratch_shapes=[
            pltpu.VMEM((SUBLANES + tm, D_RNN), F32),
            pltpu.VMEM((1, D_RNN), F32),
        ],
        compiler_params=_params(2),
        name="lru_prompt",
    )(x, g, win, cw, cb, wax, ba, bx, lam, wout)


def _lru_sample_kernel(x_ref, h0_ref, c0_ref, g_ref, win_ref, cw_ref, cb_ref, wax_ref, ba_ref, bx_ref, lam_ref,
                       wout_ref, o_ref, h_ref, c_ref):
    x = x_ref[...]
    xb = _rms(x, g_ref[...]).astype(BF16)
    u_in = _dot(xb, win_ref[...])
    xr = u_in[:, D_RNN:]
    cw = cw_ref[...]
    xc = c0_ref[0] * cw[0:1, :]
    xc = xc + c0_ref[1] * cw[1:2, :]
    xc = xc + c0_ref[2] * cw[2:3, :]
    xc = xc + xr * cw[3:4, :]
    xc = xc + cb_ref[...]
    c_ref[0] = c0_ref[1]
    c_ref[1] = c0_ref[2]
    c_ref[2] = xr

    sp = _softplus(-lam_ref[...])
    acc = x
    for k in range(LRU_BLOCKS):
        cs = slice(k * LRU_BW, (k + 1) * LRU_BW)
        ga = _dot(xc[:, cs].astype(BF16), wax_ref[k])
        a, u = _lru_gates(xc[:, cs], ga, ba_ref[:, cs], bx_ref[:, cs], sp[:, cs])
        h = a * h0_ref[:, cs] + u
        h_ref[:, cs] = h
        y = h * jax.nn.gelu(u_in[:, cs])
        acc = acc + _dot(y.astype(BF16), wout_ref[cs, :])
    o_ref[...] = acc


def _lru_sample(x, h0, c0, g, win, cw, cb, wax, ba, bx, lam, wout):
    b = x.shape[0]
    shapes = [(b, D_MODEL), (b, D_RNN), (CONV_W - 1, b, D_RNN), (1, D_MODEL), (D_MODEL, 2 * D_RNN),
              (CONV_W, D_RNN), (1, D_RNN), (LRU_BLOCKS, LRU_BW, 2 * LRU_BW), (1, D_RNN), (1, D_RNN), (1, D_RNN),
              (D_RNN, D_MODEL)]
    return pl.pallas_call(
        _lru_sample_kernel,
        grid=(1,),
        in_specs=[_const_spec(s) for s in shapes],
        out_specs=[
            pl.BlockSpec((b, D_MODEL), lambda i: (0, 0)),
            pl.BlockSpec((b, D_RNN), lambda i: (0, 0)),
            pl.BlockSpec((CONV_W - 1, b, D_RNN), lambda i: (0, 0, 0)),
        ],
        out_shape=[
            jax.ShapeDtypeStruct((b, D_MODEL), F32),
            jax.ShapeDtypeStruct((b, D_RNN), F32),
            jax.ShapeDtypeStruct((CONV_W - 1, b, D_RNN), F32),
        ],
        compiler_params=_params(1),
        name="lru_sample",
    )(x, h0, c0, g, win, cw, cb, wax, ba, bx, lam, wout)


def _head_norm(z, g):
    return z * lax.rsqrt(jnp.mean(z * z, axis=-1, keepdims=True) + RMS_EPS) * g


def _heads_norm(z, ones, g):
    z2 = z * z
    hi = z2.astype(BF16)
    lo = (z2 - hi.astype(F32)).astype(BF16)
    ssq = _dot(hi, ones) + _dot(lo, ones)
    return z * lax.rsqrt(ssq * (1.0 / HEAD_DIM) + RMS_EPS) * g


def _swa_prompt_kernel(sink_ref, slope_ref, x_ref, g_ref, wqkv_ref, qg_ref, kg_ref, ones_ref, wout_ref,
                       o_ref, k_ref, v_ref, k2, v2, qn, obuf, bias):
    tm = x_ref.shape[1]
    nblk = tm // WINDOW
    b = pl.program_id(0)
    j = pl.program_id(1)

    @pl.when((b == 0) & (j == 0))
    def _():
        qi = lax.broadcasted_iota(jnp.int32, (WINDOW, 2 * WINDOW), 0)
        si = lax.broadcasted_iota(jnp.int32, (WINDOW, 2 * WINDOW), 1)
        dist = qi + WINDOW - si
        valid = (dist >= 0) & (dist < WINDOW)
        dist_f = dist.astype(F32)
        for h in range(N_HEADS):
            bias[h] = jnp.where(valid, -(slope_ref[h] * dist_f), NEG_INF)

    @pl.when(j == 0)
    def _():
        k2[:, :, 0:WINDOW, :] = jnp.zeros((N_KV, 2, WINDOW, 2 * HEAD_DIM), BF16)
        v2[:, :, 0:WINDOW, :] = jnp.zeros((N_KV, 2, WINDOW, 2 * HEAD_DIM), BF16)

    x = x_ref[0]
    xb = _rms(x, g_ref[...]).astype(BF16)
    qkv = _dot(xb, wqkv_ref[...])

    ones = ones_ref[...]
    kn = _heads_norm(qkv[:, Q_DIM:Q_DIM + KV_DIM], ones, kg_ref[...])
    vv = qkv[:, Q_DIM + KV_DIM:]
    k_ref[0] = kn[tm - WINDOW:, :]
    v_ref[0] = vv[tm - WINDOW:, :]
    low = lax.broadcasted_iota(jnp.int32, (tm, 2 * HEAD_DIM), 1) < HEAD_DIM
    for src, dst in ((kn, k2), (vv, v2)):
        for sl in range(N_KV // 2):
            z = src[:, sl * 2 * HEAD_DIM:(sl + 1) * 2 * HEAD_DIM]
            zs = pltpu.roll(z, HEAD_DIM, 1)
            dst[2 * sl, 0, WINDOW:WINDOW + tm, :] = jnp.where(low, z, 0.0).astype(BF16)
            dst[2 * sl, 1, WINDOW:WINDOW + tm, :] = jnp.where(low, 0.0, zs).astype(BF16)
            dst[2 * sl + 1, 0, WINDOW:WINDOW + tm, :] = jnp.where(low, zs, 0.0).astype(BF16)
            dst[2 * sl + 1, 1, WINDOW:WINDOW + tm, :] = jnp.where(low, 0.0, z).astype(BF16)
    qg = qg_ref[...] * (HEAD_DIM ** -0.5)
    for kv in range(N_KV):
        cs = slice(kv * KV_DIM, (kv + 1) * KV_DIM)
        qn[:, cs] = _heads_norm(qkv[:, cs], ones, qg).astype(BF16)

    col = lax.broadcasted_iota(jnp.int32, (1, 2 * WINDOW), 1)
    no_prev = jnp.where((col < WINDOW) & (j == 0), NEG_INF, 0.0)

    def scores(kv, i):
        rq = slice(i * WINDOW, (i + 1) * WINDOW)
        ql = jnp.concatenate([qn[rq, kv * KV_DIM + p * 2 * HEAD_DIM:kv * KV_DIM + (p + 1) * 2 * HEAD_DIM]
                              for p in range(GROUP // 2)], axis=0)
        return [_dot_nt(ql, k2[kv, c, i * WINDOW:(i + 2) * WINDOW, :]) for c in range(2)]

    work = [(kv, i) for kv in range(N_KV) for i in range(nblk)]
    s_next = scores(*work[0])
    for n, (kv, i) in enumerate(work):
        if True:
            rq = slice(i * WINDOW, (i + 1) * WINDOW)
            rk = slice(i * WINDOW, (i + 2) * WINDOW)
            s = s_next
            if n + 1 < len(work):
                s_next = scores(*work[n + 1])
            probs = [[None, None], [None, None]]
            for g in range(GROUP):
                p, c = divmod(g, 2)
                h = kv * GROUP + g
                sink = sink_ref[h]
                sg = s[c][p * WINDOW:(p + 1) * WINDOW, :] + bias[h]
                if i == 0:
                    sg = sg + no_prev
                m = jnp.maximum(jnp.max(sg, axis=-1, keepdims=True), sink)
                e = jnp.exp(sg - m)
                pr = e / (jnp.sum(e, axis=-1, keepdims=True) + jnp.exp(sink - m))
                probs[c][p] = pr.astype(BF16)
            pv = (_dot(jnp.concatenate(probs[0], axis=0), v2[kv, 0, rk, :])
                  + _dot(jnp.concatenate(probs[1], axis=0), v2[kv, 1, rk, :]))
            for p in range(GROUP // 2):
                obuf[rq, kv * KV_DIM + p * 2 * HEAD_DIM:kv * KV_DIM + (p + 1) * 2 * HEAD_DIM] = (
                    pv[p * WINDOW:(p + 1) * WINDOW, :].astype(BF16))

    k2[:, :, 0:WINDOW, :] = k2[:, :, tm:tm + WINDOW, :]
    v2[:, :, 0:WINDOW, :] = v2[:, :, tm:tm + WINDOW, :]
    o_ref[0] = x + _dot(obuf[...], wout_ref[...])


def _swa_prompt(x, sinks, slopes, g, wqkv, qg4, kg4, ones, wout):
    n, t, _ = x.shape
    tm = SWA_TM
    smem = pl.BlockSpec(memory_space=pltpu.SMEM)
    return pl.pallas_call(
        _swa_prompt_kernel,
        grid=(n, t // tm),
        in_specs=[
            smem,
            smem,
            pl.BlockSpec((1, tm, D_MODEL), lambda b, j: (b, j, 0)),
            _const_spec((1, D_MODEL)),
            _const_spec((D_MODEL, QKV_DIM)),
            _const_spec((1, KV_DIM)),
            _const_spec((1, KV_DIM)),
            _const_spec((KV_DIM, KV_DIM)),
            _const_spec((Q_DIM, D_MODEL)),
        ],
        out_specs=[
            pl.BlockSpec((1, tm, D_MODEL), lambda b, j: (b, j, 0)),
            pl.BlockSpec((1, WINDOW, KV_DIM), lambda b, j: (b, 0, 0)),
            pl.BlockSpec((1, WINDOW, KV_DIM), lambda b, j: (b, 0, 0)),
        ],
        out_shape=[
            jax.ShapeDtypeStruct((n, t, D_MODEL), F32),
            jax.ShapeDtypeStruct((n, WINDOW, KV_DIM), F32),
            jax.ShapeDtypeStruct((n, WINDOW, KV_DIM), F32),
        ],
        scratch_shapes=[
            pltpu.VMEM((N_KV, 2, WINDOW + tm, 2 * HEAD_DIM), BF16),
            pltpu.VMEM((N_KV, 2, WINDOW + tm, 2 * HEAD_DIM), BF16),
            pltpu.VMEM((tm, Q_DIM), BF16),
            pltpu.VMEM((tm, Q_DIM), BF16),
            pltpu.VMEM((N_HEADS, WINDOW, 2 * WINDOW), F32),
        ],
        compiler_params=_params(2),
        name="swa_prompt",
    )(sinks, slopes, x, g, wqkv, qg4, kg4, ones, wout)


def _swa_sample_qkv_kernel(x_ref, g_ref, wqkv_ref, wkvt_ref, qg_ref, kg_ref, kgt_ref,
                           q_ref, k_ref, v_ref, kt_ref, vt_ref):
    xb = _rms(x_ref[...], g_ref[...]).astype(BF16)
    qkv = _dot(xb, wqkv_ref[...])
    for h in range(N_HEADS):
        cs = slice(h * HEAD_DIM, (h + 1) * HEAD_DIM)
        q_ref[:, cs] = _head_norm(qkv[:, cs], qg_ref[...])
    for kv in range(N_KV):
        cs = slice(kv * HEAD_DIM, (kv + 1) * HEAD_DIM)
        k_ref[:, cs] = _head_norm(qkv[:, Q_DIM + kv * HEAD_DIM:Q_DIM + (kv + 1) * HEAD_DIM], kg_ref[...])
    v_ref[...] = qkv[:, Q_DIM + KV_DIM:]
    kvt = _dot_nt(wkvt_ref[...], xb)
    for kv in range(N_KV):
        rs = slice(kv * HEAD_DIM, (kv + 1) * HEAD_DIM)
        z = kvt[rs, :]
        kt_ref[rs, :] = z * lax.rsqrt(jnp.mean(z * z, axis=0, keepdims=True) + RMS_EPS) * kgt_ref[...]
    vt_ref[...] = kvt[KV_DIM:, :]


def _swa_sample_qkv(x, g, wqkv, wkvt, qg, kg, kgt):
    b = x.shape[0]
    shapes = [(b, D_MODEL), (1, D_MODEL), (D_MODEL, QKV_DIM), (2 * KV_DIM, D_MODEL), (1, HEAD_DIM), (1, HEAD_DIM),
              (HEAD_DIM, b)]
    out_shapes = [(b, Q_DIM), (b, KV_DIM), (b, KV_DIM), (KV_DIM, b), (KV_DIM, b)]
    return pl.pallas_call(
        _swa_sample_qkv_kernel,
        grid=(1,),
        in_specs=[_const_spec(s) for s in shapes],
        out_specs=[pl.BlockSpec(s, lambda i: (0, 0)) for s in out_shapes],
        out_shape=[jax.ShapeDtypeStruct(s, F32) for s in out_shapes],
        compiler_params=_params(1),
        name="swa_sample_qkv",
    )(x, g, wqkv, wkvt, qg, kg, kgt)


def _swa_sample_attn_kernel(q_ref, kn_ref, vn_ref, knt_ref, vnt_ref, kc_ref, vc_ref, sink_ref, slope_ref,
                            o_ref, ko_ref, vo_ref):
    bs = q_ref.shape[0]
    w = kc_ref.shape[3]
    i = pl.program_id(0)
    hrow = lax.broadcasted_iota(jnp.int32, (N_HEADS, KV_DIM), 0)
    feat = lax.broadcasted_iota(jnp.int32, (N_HEADS, KV_DIM), 1)
    own = (hrow // GROUP) == (feat // HEAD_DIM)
    key = lax.broadcasted_iota(jnp.int32, (N_HEADS, w), 1)
    dist = w - key
    bias = jnp.where(dist < WINDOW, -(slope_ref[:, 0:1] * dist.astype(F32)), NEG_INF)
    sink = sink_ref[:, 0:1]
    lane = lax.broadcasted_iota(jnp.int32, (KV_DIM, w), 1)
    newest = lane == w - 1

    def scores(b):
        q = q_ref[b]
        qt = jnp.concatenate([q] * N_KV, axis=1)
        qm = (jnp.where(own, qt, 0.0) * (HEAD_DIM ** -0.5)).astype(BF16)
        return qm, _dot(qm, kc_ref[0, b].astype(BF16)) + bias

    def softmax(b, qm, s):
        kn = kn_ref[b].astype(BF16).astype(F32)
        s_new = jnp.sum(qm.astype(F32) * kn, axis=-1, keepdims=True)
        m = jnp.maximum(jnp.maximum(jnp.max(s, axis=-1, keepdims=True), s_new), sink)
        e = jnp.exp(s - m)
        e_new = jnp.exp(s_new - m)
        den = jnp.sum(e, axis=-1, keepdims=True) + e_new + jnp.exp(sink - m)
        return (e / den).astype(BF16), (e_new / den).astype(BF16).astype(F32)

    def values(b, p, p_new):
        vn = vn_ref[b].astype(BF16).astype(F32)
        return _dot_nt(p, vc_ref[0, b].astype(BF16)) + p_new * vn

    def emit(b, res):
        res = jnp.where(own, res, 0.0)
        o = res[:, 0:HEAD_DIM]
        for kv in range(1, N_KV):
            o = o + res[:, kv * HEAD_DIM:(kv + 1) * HEAD_DIM]
        o_ref[b] = o

    def shift_in(b):
        mine = lane == i * bs + b
        kcol = jnp.sum(jnp.where(mine, knt_ref[...], 0.0), axis=1, keepdims=True)
        vcol = jnp.sum(jnp.where(mine, vnt_ref[...], 0.0), axis=1, keepdims=True)
        ko_ref[b] = jnp.where(newest, kcol, pltpu.roll(kc_ref[0, b], w - 1, 1))
        vo_ref[b] = jnp.where(newest, vcol, pltpu.roll(vc_ref[0, b], w - 1, 1))

    def body(grp, carry):
        seqs = [grp * SWA_SAMPLE_UNROLL + u for u in range(SWA_SAMPLE_UNROLL)]
        staged = [scores(b) for b in seqs]
        probs = [softmax(b, qm, s) for b, (qm, s) in zip(seqs, staged)]
        for b in seqs:
            shift_in(b)
        outs = [values(b, p, p_new) for b, (p, p_new) in zip(seqs, probs)]
        for b, res in zip(seqs, outs):
            emit(b, res)
        return carry

    lax.fori_loop(0, bs // SWA_SAMPLE_UNROLL, body, 0)


def _swa_sample_attn(layer, q3, kn3, vn3, knt, vnt, kct, vct, sink_b, slope_b):
    _, b, _, w = kct.shape
    assert w == 128 and b == 128, "one lane tile of key positions; new-token columns indexed by lane"
    bs = SWA_SAMPLE_BS
    cache_spec = pl.BlockSpec((1, bs, KV_DIM, w), lambda i: (layer, i, 0, 0))
    return pl.pallas_call(
        _swa_sample_attn_kernel,
        grid=(b // bs,),
        in_specs=[
            pl.BlockSpec((bs, N_HEADS, HEAD_DIM), lambda i: (i, 0, 0)),
            pl.BlockSpec((bs, 1, KV_DIM), lambda i: (i, 0, 0)),
            pl.BlockSpec((bs, 1, KV_DIM), lambda i: (i, 0, 0)),
            _const_spec((KV_DIM, b)),
            _const_spec((KV_DIM, b)),
            cache_spec,
            cache_spec,
            _const_spec((N_HEADS, 128)),
            _const_spec((N_HEADS, 128)),
        ],
        out_specs=[
            pl.BlockSpec((bs, N_HEADS, HEAD_DIM), lambda i: (i, 0, 0)),
            pl.BlockSpec((bs, KV_DIM, w), lambda i: (i, 0, 0)),
            pl.BlockSpec((bs, KV_DIM, w), lambda i: (i, 0, 0)),
        ],
        out_shape=[
            jax.ShapeDtypeStruct((b, N_HEADS, HEAD_DIM), F32),
            jax.ShapeDtypeStruct((b, KV_DIM, w), F32),
            jax.ShapeDtypeStruct((b, KV_DIM, w), F32),
        ],
        compiler_params=_params(1),
        name="swa_sample_attn",
    )(q3, kn3, vn3, knt, vnt, kct, vct, sink_b, slope_b)


def _proj_residual_kernel(x_ref, o_ref_in, wout_ref, y_ref):
    y_ref[...] = x_ref[...] + _dot(o_ref_in[...].astype(BF16), wout_ref[...])


def _proj_residual(x, o, wout):
    b = x.shape[0]
    shapes = [(b, D_MODEL), (b, Q_DIM), (Q_DIM, D_MODEL)]
    return pl.pallas_call(
        _proj_residual_kernel,
        grid=(1,),
        in_specs=[_const_spec(s) for s in shapes],
        out_specs=pl.BlockSpec((b, D_MODEL), lambda i: (0, 0)),
        out_shape=jax.ShapeDtypeStruct((b, D_MODEL), F32),
        compiler_params=_params(1),
        name="proj_residual",
    )(x, o, wout)


def kernel(x_prompt, x_sample, state_rglru_h, state_rglru_conv, cache_swa_k, cache_swa_v, norm_mix_g, norm_mlp_g, lru_w_in, lru_conv_w, lru_conv_b, lru_w_a, lru_b_a, lru_w_x, lru_b_x, lru_lambda, lru_w_out, attn_w_qkv, attn_q_norm, attn_k_norm, attn_sinks, attn_w_out, mlp_w_up, mlp_w_down):
    n_p, t_p, _ = x_prompt.shape
    n_s = x_sample.shape[0]
    w_buf = cache_swa_k.shape[2]
    heads = jnp.arange(1, N_HEADS + 1, dtype=F32)
    slopes = jnp.exp2(-8.0 * heads / N_HEADS)
    slopes_b = jnp.broadcast_to(slopes[:, None], (N_HEADS, 128))
    head_of_lane = jnp.arange(KV_DIM) // HEAD_DIM
    head_ones = (head_of_lane[:, None] == head_of_lane[None, :]).astype(BF16)

    to_feature_major = lambda c: c.transpose(0, 1, 3, 4, 2).reshape(c.shape[0], n_s, KV_DIM, w_buf)
    from_feature_major = lambda c: c.reshape(c.shape[0], n_s, N_KV, HEAD_DIM, w_buf).transpose(0, 1, 4, 2, 3)
    kct, vct = to_feature_major(cache_swa_k), to_feature_major(cache_swa_v)

    row = lambda v: v.reshape(1, -1)
    yp = x_prompt
    ys = x_sample.reshape(n_s, D_MODEL)
    h_p, c_p, k_p, v_p, h_s, c_s, k_s, v_s = ([] for _ in range(8))
    for layer in range(DEPTH):
        j = layer // 2
        g_mix = row(norm_mix_g[layer])
        if layer % 2 == 0:
            win = lru_w_in[j].astype(BF16)
            wax = jnp.concatenate([lru_w_a[j], lru_w_x[j]], axis=-1).astype(BF16)
            args = (g_mix, win, lru_conv_w[j], row(lru_conv_b[j]), wax, row(lru_b_a[j]), row(lru_b_x[j]),
                    row(lru_lambda[j]), lru_w_out[j].astype(BF16))
            yp, hp, cp = _lru_prompt(yp, *args)
            ys, hs, cs = _lru_sample(ys, state_rglru_h[j], state_rglru_conv[j].transpose(1, 0, 2), *args)
            h_p.append(hp.reshape(n_p, D_RNN)); c_p.append(cp)
            h_s.append(hs); c_s.append(cs.transpose(1, 0, 2))
        else:
            wqkv = attn_w_qkv[j].astype(BF16)
            wout = attn_w_out[j].astype(BF16)
            qg, kg = row(attn_q_norm[j]), row(attn_k_norm[j])
            sinks = attn_sinks[j]
            yp, kp, vp = _swa_prompt(yp, sinks, slopes, g_mix, wqkv, jnp.tile(qg, (1, GROUP)),
                                     jnp.tile(kg, (1, N_KV)), head_ones, wout)
            k_p.append(kp.reshape(n_p, w_buf, N_KV, HEAD_DIM)); v_p.append(vp.reshape(n_p, w_buf, N_KV, HEAD_DIM))
            wkvt = attn_w_qkv[j][:, Q_DIM:].T.astype(BF16)
            kgt = jnp.broadcast_to(attn_k_norm[j][:, None], (HEAD_DIM, n_s))
            q, kn, vn, knt, vnt = _swa_sample_qkv(ys, g_mix, wqkv, wkvt, qg, kg, kgt)
            o3, ks, vs = _swa_sample_attn(
                j, q.reshape(n_s, N_HEADS, HEAD_DIM), kn.reshape(n_s, 1, KV_DIM), vn.reshape(n_s, 1, KV_DIM),
                knt, vnt, kct, vct, jnp.broadcast_to(sinks[:, None], (N_HEADS, 128)), slopes_b)
            ys = _proj_residual(ys, o3.reshape(n_s, Q_DIM), wout)
            k_s.append(ks); v_s.append(vs)
        g_mlp = row(norm_mlp_g[layer])
        wu = mlp_w_up[layer].astype(BF16)
        wd = mlp_w_down[layer].astype(BF16)
        yp = _mlp(yp.reshape(n_p * t_p, D_MODEL), g_mlp, wu, wd, MLP_TM).reshape(n_p, t_p, D_MODEL)
        ys = _mlp(ys, g_mlp, wu, wd, n_s)
    return (yp, ys.reshape(n_s, 1, D_MODEL),
            jnp.stack(h_p), jnp.stack(c_p), jnp.stack(k_p), jnp.stack(v_p),
            jnp.stack(h_s), jnp.stack(c_s), from_feature_major(jnp.stack(k_s)), from_feature_major(jnp.stack(v_s)))
```

```python
import functools

import jax
import jax.numpy as jnp
from jax import lax
from jax.experimental import pallas as pl
from jax.experimental.pallas import tpu as pltpu

F32 = jnp.float32
BF16 = jnp.bfloat16

D_MODEL = 1024
D_RNN = 1024
D_FF = 4096
DEPTH = 4
CONV_W = 4
LRU_BLOCKS = 4
LRU_BW = D_RNN // LRU_BLOCKS
LRU_C = 8.0
N_HEADS = 16
N_KV = 4
GROUP = N_HEADS // N_KV
HEAD_DIM = 64
Q_DIM = N_HEADS * HEAD_DIM
KV_DIM = N_KV * HEAD_DIM
QKV_DIM = Q_DIM + 2 * KV_DIM
WINDOW = 128
assert GROUP * HEAD_DIM == KV_DIM and N_KV % 2 == 0 and GROUP % 2 == 0
RMS_EPS = 1e-6
NEG_INF = -1e30

V7X_VMEM_LIMIT_BYTES = 56 * 1024 * 1024
SUBLANES = 8

MLP_TM = 512
MLP_FC = 1024
LRU_TM = 256
SWA_TM = 256
SWA_SAMPLE_BS = 16
SWA_SAMPLE_UNROLL = 8


def _params(n_axes):
    return pltpu.CompilerParams(
        dimension_semantics=("arbitrary",) * n_axes,
        vmem_limit_bytes=V7X_VMEM_LIMIT_BYTES,
    )


def _const_spec(shape):
    nd = len(shape)
    return pl.BlockSpec(shape, lambda *_: (0,) * nd, pipeline_mode=pl.Buffered(1))


def _rms(x, g):
    return x * lax.rsqrt(jnp.mean(x * x, axis=-1, keepdims=True) + RMS_EPS) * g


def _dot(a, b):
    return jnp.dot(a, b, preferred_element_type=F32)


def _dot_nt(a, b):
    return lax.dot_general(a, b, (((1,), (1,)), ((), ())), preferred_element_type=F32)


def _layer_spec(shape, layer):
    nd = len(shape)
    return pl.BlockSpec((None,) + shape, lambda *_: (layer,) + (0,) * nd, pipeline_mode=pl.Buffered(1))


def _mlp_kernel(xp_ref, xs_ref, g_ref, wu_ref, wd_ref, op_ref, os_ref):
    def run(x_ref, o_ref):
        x = x_ref[...]
        xb = _rms(x, g_ref[...]).astype(BF16)
        acc = x
        for c in range(D_FF // MLP_FC):
            h = _dot(xb, wu_ref[:, c * MLP_FC:(c + 1) * MLP_FC])
            h = jnp.square(jnp.maximum(h, 0.0)).astype(BF16)
            acc = acc + _dot(h, wd_ref[c * MLP_FC:(c + 1) * MLP_FC, :])
        o_ref[...] = acc

    last = pl.num_programs(0) - 1
    pl.when(pl.program_id(0) < last)(lambda: run(xp_ref, op_ref))
    pl.when(pl.program_id(0) == last)(lambda: run(xs_ref, os_ref))


def _mlp(xp, xs, g, wu_all, wd_all, layer):
    rows, rows_s, tm = xp.shape[0], xs.shape[0], MLP_TM
    nblk = rows // tm
    prompt_spec = pl.BlockSpec((tm, D_MODEL), lambda i: (jnp.minimum(i, nblk - 1), 0))
    sample_spec = pl.BlockSpec((rows_s, D_MODEL), lambda i: (0, 0))
    return pl.pallas_call(
        _mlp_kernel,
        grid=(nblk + 1,),
        in_specs=[
            prompt_spec,
            sample_spec,
            _const_spec((1, D_MODEL)),
            _layer_spec((D_MODEL, D_FF), layer),
            _layer_spec((D_FF, D_MODEL), layer),
        ],
        out_specs=[prompt_spec, sample_spec],
        out_shape=[jax.ShapeDtypeStruct((rows, D_MODEL), F32), jax.ShapeDtypeStruct((rows_s, D_MODEL), F32)],
        compiler_params=_params(1),
        name="mlp",
    )(xp, xs, g, wu_all, wd_all)


def _softplus(x):
    return jnp.maximum(x, 0.0) + jnp.log1p(jnp.exp(-jnp.abs(x)))


def _lru_gates(xc_k, ga, ba_k, bx_k, sp_k):
    r = jax.nn.sigmoid(ga[:, :LRU_BW] + ba_k)
    i = jax.nn.sigmoid(ga[:, LRU_BW:] + bx_k)
    log_a = -LRU_C * r * sp_k
    a = jnp.exp(log_a)
    u = jnp.sqrt(-jnp.tanh(log_a) * (a * a + 1.0)) * (i * xc_k)
    return a, u


def _scan_rows(a, u, h0):
    rows, c = a.shape
    groups = rows // SUBLANES
    a = a.reshape(groups, SUBLANES, c)
    u = u.reshape(groups, SUBLANES, c)
    row = lax.broadcasted_iota(jnp.int32, a.shape, 1)
    d = 1
    while d < SUBLANES:
        keep = row >= d
        u = jnp.where(keep, u + a * pltpu.roll(u, d, 1), u)
        a = jnp.where(keep, a * pltpu.roll(a, d, 1), a)
        d *= 2
    hs = []
    h = h0
    for g in range(groups):
        hg = a[g] * h + u[g]
        hs.append(hg)
        h = hg[SUBLANES - 1:SUBLANES, :]
    return jnp.concatenate(hs, axis=0)


def _lru_prompt_kernel(x_ref, g_ref, win_ref, cw_ref, cb_ref, wax_ref, ba_ref, bx_ref, lam_ref, wout_ref,
                       o_ref, h_ref, c_ref, xbuf, hcar):
    tm = x_ref.shape[1]
    j = pl.program_id(1)

    @pl.when(j == 0)
    def _():
        xbuf[0:SUBLANES, :] = jnp.zeros((SUBLANES, D_RNN), F32)
        hcar[...] = jnp.zeros_like(hcar)

    x = x_ref[0]
    xb = _rms(x, g_ref[...]).astype(BF16)
    u_in = _dot(xb, win_ref[...])
    xr = u_in[:, D_RNN:]
    xbuf[SUBLANES:SUBLANES + tm, :] = xr
    cw = cw_ref[...]
    xc = xbuf[pl.ds(SUBLANES - 3, tm), :] * cw[0:1, :]
    xc = xc + xbuf[pl.ds(SUBLANES - 2, tm), :] * cw[1:2, :]
    xc = xc + xbuf[pl.ds(SUBLANES - 1, tm), :] * cw[2:3, :]
    xc = xc + xr * cw[3:4, :]
    xc = xc + cb_ref[...]
    c_ref[0] = xbuf[pl.ds(SUBLANES + tm - (CONV_W - 1), CONV_W - 1), :]
    xbuf[0:SUBLANES, :] = xbuf[tm:tm + SUBLANES, :]

    sp = _softplus(-lam_ref[...])
    blocks = [slice(k * LRU_BW, (k + 1) * LRU_BW) for k in range(LRU_BLOCKS)]
    gates = [_dot(xc[:, cs].astype(BF16), wax_ref[k]) for k, cs in enumerate(blocks)]
    acc = x
    for k, cs in enumerate(blocks):
        a, u = _lru_gates(xc[:, cs], gates[k], ba_ref[:, cs], bx_ref[:, cs], sp[:, cs])
        h = _scan_rows(a, u, hcar[:, cs])
        hcar[:, cs] = h[tm - 1:tm, :]
        y = h * jax.nn.gelu(u_in[:, cs])
        acc = acc + _dot(y.astype(BF16), wout_ref[cs, :])
    o_ref[0] = acc
    h_ref[0] = hcar[...]


def _lru_prompt(layer, x, g, win, cw, cb, wax, ba, bx, lam, wout):
    n, t, _ = x.shape
    tm = LRU_TM
    return pl.pallas_call(
        _lru_prompt_kernel,
        grid=(n, t // tm),
        in_specs=[
            pl.BlockSpec((1, tm, D_MODEL), lambda b, j: (b, j, 0)),
            _const_spec((1, D_MODEL)),
            _layer_spec((D_MODEL, 2 * D_RNN), layer),
            _const_spec((CONV_W, D_RNN)),
            _const_spec((1, D_RNN)),
            _layer_spec((LRU_BLOCKS, LRU_BW, 2 * LRU_BW), layer),
            _const_spec((1, D_RNN)),
            _const_spec((1, D_RNN)),
            _const_spec((1, D_RNN)),
            _layer_spec((D_RNN, D_MODEL), layer),
        ],
        out_specs=[
            pl.BlockSpec((1, tm, D_MODEL), lambda b, j: (b, j, 0)),
            pl.BlockSpec((1, 1, D_RNN), lambda b, j: (b, 0, 0)),
            pl.BlockSpec((1, CONV_W - 1, D_RNN), lambda b, j: (b, 0, 0)),
        ],
        out_shape=[
            jax.ShapeDtypeStruct((n, t, D_MODEL), F32),
            jax.ShapeDtypeStruct((n, 1, D_RNN), F32),
            jax.ShapeDtypeStruct((n, CONV_W - 1, D_RNN), F32),
        ],
        scratch_shapes=[
            pltpu.VMEM((SUBLANES + tm, D_RNN), F32),
            pltpu.VMEM((1, D_RNN), F32),
        ],
        compiler_params=_params(2),
        name="lru_prompt",
    )(x, g, win, cw, cb, wax, ba, bx, lam, wout)


def _lru_mlp_kernel(n_seq, x_ref, xs_ref, g_ref, win_ref, cw_ref, cb_ref, wax_ref, ba_ref, bx_ref, lam_ref,
                    wout_ref, g2_ref, wu_ref, wd_ref, o_ref, os_ref, h_ref, c_ref, xbuf, hcar, ymid):
    tm = x_ref.shape[1]
    nj = (pl.num_programs(0) - 1) // n_seq - 1
    i = pl.program_id(0)
    j = lax.rem(i, nj + 1)

    def mlp_chunk(xb, c):
        h = _dot(xb, wu_ref[:, c * MLP_FC:(c + 1) * MLP_FC])
        return jnp.square(jnp.maximum(h, 0.0)).astype(BF16)

    @pl.when(i < n_seq * (nj + 1))
    def _():
        @pl.when(j == 0)
        def _():
            xbuf[0:SUBLANES, :] = jnp.zeros((SUBLANES, D_RNN), F32)
            hcar[...] = jnp.zeros_like(hcar)
            ymid[...] = jnp.zeros_like(ymid)

        x = x_ref[0]
        xb = _rms(x, g_ref[...]).astype(BF16)
        u_in = _dot(xb, win_ref[...])
        yprev = ymid[...]
        yb = _rms(yprev, g2_ref[...]).astype(BF16)
        xr = u_in[:, D_RNN:]
        xbuf[SUBLANES:SUBLANES + tm, :] = xr
        cw = cw_ref[...]
        xc = xbuf[pl.ds(SUBLANES - 3, tm), :] * cw[0:1, :]
        xc = xc + xbuf[pl.ds(SUBLANES - 2, tm), :] * cw[1:2, :]
        xc = xc + xbuf[pl.ds(SUBLANES - 1, tm), :] * cw[2:3, :]
        xc = xc + xr * cw[3:4, :]
        xc = xc + cb_ref[...]
        xbuf[0:SUBLANES, :] = xbuf[tm:tm + SUBLANES, :]

        sp = _softplus(-lam_ref[...])
        blocks = [slice(k * LRU_BW, (k + 1) * LRU_BW) for k in range(LRU_BLOCKS)]
        gates = [_dot(xc[:, cs].astype(BF16), wax_ref[k]) for k, cs in enumerate(blocks)]
        acc = x
        mlp = yprev
        for k, cs in enumerate(blocks):
            hk = mlp_chunk(yb, k)
            a, u = _lru_gates(xc[:, cs], gates[k], ba_ref[:, cs], bx_ref[:, cs], sp[:, cs])
            h = _scan_rows(a, u, hcar[:, cs])
            hcar[:, cs] = h[tm - 1:tm, :]
            y = h * jax.nn.gelu(u_in[:, cs])
            mlp = mlp + _dot(hk, wd_ref[k * MLP_FC:(k + 1) * MLP_FC, :])
            acc = acc + _dot(y.astype(BF16), wout_ref[cs, :])
        ymid[...] = acc
        o_ref[0] = mlp

        @pl.when(j == nj - 1)
        def _():
            h_ref[0] = hcar[...]
            c_ref[0] = xbuf[pl.ds(SUBLANES - (CONV_W - 1), CONV_W - 1), :]

    @pl.when(i == n_seq * (nj + 1))
    def _():
        x = xs_ref[...]
        xb = _rms(x, g2_ref[...]).astype(BF16)
        acc = x
        for c in range(D_FF // MLP_FC):
            acc = acc + _dot(mlp_chunk(xb, c), wd_ref[c * MLP_FC:(c + 1) * MLP_FC, :])
        os_ref[...] = acc


def _lru_mlp(layer, mlp_layer, x, xs, g, win, cw, cb, wax, ba, bx, lam, wout, g2, wu, wd):
    n, t, _ = x.shape
    rows_s = xs.shape[0]
    tm = LRU_TM
    assert D_FF // MLP_FC == LRU_BLOCKS
    nj = t // tm
    steps = nj + 1
    seq = lambda i: jnp.minimum(i // steps, n - 1)
    chunk_in = lambda i: jnp.where(i >= n * steps, nj - 1, jnp.minimum(lax.rem(i, steps), nj - 1))
    chunk_out = lambda i: jnp.where(i >= n * steps, nj - 1, jnp.maximum(lax.rem(i, steps) - 1, 0))
    sample_spec = pl.BlockSpec((rows_s, D_MODEL), lambda i: (0, 0))
    return pl.pallas_call(
        functools.partial(_lru_mlp_kernel, n),
        grid=(n * steps + 1,),
        in_specs=[
            pl.BlockSpec((1, tm, D_MODEL), lambda i: (seq(i), chunk_in(i), 0)),
            sample_spec,
            _const_spec((1, D_MODEL)),
            _layer_spec((D_MODEL, 2 * D_RNN), layer),
            _const_spec((CONV_W, D_RNN)),
            _const_spec((1, D_RNN)),
            _layer_spec((LRU_BLOCKS, LRU_BW, 2 * LRU_BW), layer),
            _const_spec((1, D_RNN)),
            _const_spec((1, D_RNN)),
            _const_spec((1, D_RNN)),
            _layer_spec((D_RNN, D_MODEL), layer),
            _const_spec((1, D_MODEL)),
            _layer_spec((D_MODEL, D_FF), mlp_layer),
            _layer_spec((D_FF, D_MODEL), mlp_layer),
        ],
        out_specs=[
            pl.BlockSpec((1, tm, D_MODEL), lambda i: (seq(i), chunk_out(i), 0)),
            sample_spec,
            pl.BlockSpec((1, 1, D_RNN), lambda i: (seq(i), 0, 0)),
            pl.BlockSpec((1, CONV_W - 1, D_RNN), lambda i: (seq(i), 0, 0)),
        ],
        out_shape=[
            jax.ShapeDtypeStruct((n, t, D_MODEL), F32),
            jax.ShapeDtypeStruct((rows_s, D_MODEL), F32),
            jax.ShapeDtypeStruct((n, 1, D_RNN), F32),
            jax.ShapeDtypeStruct((n, CONV_W - 1, D_RNN), F32),
        ],
        scratch_shapes=[
            pltpu.VMEM((SUBLANES + tm, D_RNN), F32),
            pltpu.VMEM((1, D_RNN), F32),
            pltpu.VMEM((tm, D_MODEL), F32),
        ],
        compiler_params=_params(1),
        name="lru_mlp",
    )(x, xs, g, win, cw, cb, wax, ba, bx, lam, wout, g2, wu, wd)


def _lru_sample_kernel(x_ref, h0_ref, c0_ref, g_ref, win_ref, cw_ref, cb_ref, wax_ref, ba_ref, bx_ref, lam_ref,
                       wout_ref, o_ref, h_ref, c_ref):
    x = x_ref[...]
    xb = _rms(x, g_ref[...]).astype(BF16)
    u_in = _dot(xb, win_ref[...])
    xr = u_in[:, D_RNN:]
    cw = cw_ref[...]
    xc = c0_ref[0] * cw[0:1, :]
    xc = xc + c0_ref[1] * cw[1:2, :]
    xc = xc + c0_ref[2] * cw[2:3, :]
    xc = xc + xr * cw[3:4, :]
    xc = xc + cb_ref[...]
    c_ref[0] = c0_ref[1]
    c_ref[1] = c0_ref[2]
    c_ref[2] = xr

    sp = _softplus(-lam_ref[...])
    acc = x
    for k in range(LRU_BLOCKS):
        cs = slice(k * LRU_BW, (k + 1) * LRU_BW)
        ga = _dot(xc[:, cs].astype(BF16), wax_ref[k])
        a, u = _lru_gates(xc[:, cs], ga, ba_ref[:, cs], bx_ref[:, cs], sp[:, cs])
        h = a * h0_ref[:, cs] + u
        h_ref[:, cs] = h
        y = h * jax.nn.gelu(u_in[:, cs])
        acc = acc + _dot(y.astype(BF16), wout_ref[cs, :])
    o_ref[...] = acc


def _lru_sample(layer, x, h0, c0, g, win, cw, cb, wax, ba, bx, lam, wout):
    b = x.shape[0]
    stacked = {4: (D_MODEL, 2 * D_RNN), 7: (LRU_BLOCKS, LRU_BW, 2 * LRU_BW), 11: (D_RNN, D_MODEL)}
    shapes = [(b, D_MODEL), (b, D_RNN), (CONV_W - 1, b, D_RNN), (1, D_MODEL), None,
              (CONV_W, D_RNN), (1, D_RNN), None, (1, D_RNN), (1, D_RNN), (1, D_RNN), None]
    return pl.pallas_call(
        _lru_sample_kernel,
        grid=(1,),
        in_specs=[_layer_spec(stacked[k], layer) if s is None else _const_spec(s) for k, s in enumerate(shapes)],
        out_specs=[
            pl.BlockSpec((b, D_MODEL), lambda i: (0, 0)),
            pl.BlockSpec((b, D_RNN), lambda i: (0, 0)),
            pl.BlockSpec((CONV_W - 1, b, D_RNN), lambda i: (0, 0, 0)),
        ],
        out_shape=[
            jax.ShapeDtypeStruct((b, D_MODEL), F32),
            jax.ShapeDtypeStruct((b, D_RNN), F32),
            jax.ShapeDtypeStruct((CONV_W - 1, b, D_RNN), F32),
        ],
        compiler_params=_params(1),
        name="lru_sample",
    )(x, h0, c0, g, win, cw, cb, wax, ba, bx, lam, wout)


def _head_norm(z, g):
    return z * lax.rsqrt(jnp.mean(z * z, axis=-1, keepdims=True) + RMS_EPS) * g


def _heads_norm(z, ones, g):
    z2 = z * z
    hi = z2.astype(BF16)
    lo = (z2 - hi.astype(F32)).astype(BF16)
    ssq = _dot(hi, ones) + _dot(lo, ones)
    return z * lax.rsqrt(ssq * (1.0 / HEAD_DIM) + RMS_EPS) * g


def _swa_prompt_kernel(sink_ref, slope_ref, x_ref, g_ref, wqkv_ref, qg_ref, kg_ref, ones_ref, wout_ref,
                       o_ref, k_ref, v_ref, k2, v2, qn, obuf, bias):
    tm = x_ref.shape[1]
    nblk = tm // WINDOW
    b = pl.program_id(0)
    j = pl.program_id(1)

    @pl.when((b == 0) & (j == 0))
    def _():
        qi = lax.broadcasted_iota(jnp.int32, (WINDOW, 2 * WINDOW), 0)
        si = lax.broadcasted_iota(jnp.int32, (WINDOW, 2 * WINDOW), 1)
        dist = qi + WINDOW - si
        valid = (dist >= 0) & (dist < WINDOW)
        dist_f = dist.astype(F32)
        for h in range(N_HEADS):
            bias[h] = jnp.where(valid, -(slope_ref[h] * dist_f), NEG_INF)

    @pl.when(j == 0)
    def _():
        k2[:, :, 0:WINDOW, :] = jnp.zeros((N_KV, 2, WINDOW, 2 * HEAD_DIM), BF16)
        v2[:, :, 0:WINDOW, :] = jnp.zeros((N_KV, 2, WINDOW, 2 * HEAD_DIM), BF16)

    x = x_ref[0]
    xb = _rms(x, g_ref[...]).astype(BF16)
    qkv = _dot(xb, wqkv_ref[...])

    ones = ones_ref[...]
    kn = _heads_norm(qkv[:, Q_DIM:Q_DIM + KV_DIM], ones, kg_ref[...])
    vv = qkv[:, Q_DIM + KV_DIM:]
    k_ref[0] = kn[tm - WINDOW:, :]
    v_ref[0] = vv[tm - WINDOW:, :]
    low = lax.broadcasted_iota(jnp.int32, (tm, 2 * HEAD_DIM), 1) < HEAD_DIM
    for src, dst in ((kn, k2), (vv, v2)):
        for sl in range(N_KV // 2):
            z = src[:, sl * 2 * HEAD_DIM:(sl + 1) * 2 * HEAD_DIM]
            zs = pltpu.roll(z, HEAD_DIM, 1)
            dst[2 * sl, 0, WINDOW:WINDOW + tm, :] = jnp.where(low, z, 0.0).astype(BF16)
            dst[2 * sl, 1, WINDOW:WINDOW + tm, :] = jnp.where(low, 0.0, zs).astype(BF16)
            dst[2 * sl + 1, 0, WINDOW:WINDOW + tm, :] = jnp.where(low, zs, 0.0).astype(BF16)
            dst[2 * sl + 1, 1, WINDOW:WINDOW + tm, :] = jnp.where(low, 0.0, z).astype(BF16)
    qg = qg_ref[...] * (HEAD_DIM ** -0.5)
    for kv in range(N_KV):
        cs = slice(kv * KV_DIM, (kv + 1) * KV_DIM)
        qn[:, cs] = _heads_norm(qkv[:, cs], ones, qg).astype(BF16)

    col = lax.broadcasted_iota(jnp.int32, (1, 2 * WINDOW), 1)
    no_prev = jnp.where((col < WINDOW) & (j == 0), NEG_INF, 0.0)

    def scores(kv, i):
        rq = slice(i * WINDOW, (i + 1) * WINDOW)
        ql = jnp.concatenate([qn[rq, kv * KV_DIM + p * 2 * HEAD_DIM:kv * KV_DIM + (p + 1) * 2 * HEAD_DIM]
                              for p in range(GROUP // 2)], axis=0)
        return [_dot_nt(ql, k2[kv, c, i * WINDOW:(i + 2) * WINDOW, :]) for c in range(2)]

    work = [(kv, i) for kv in range(N_KV) for i in range(nblk)]
    s_next = scores(*work[0])
    for n, (kv, i) in enumerate(work):
        if True:
            rq = slice(i * WINDOW, (i + 1) * WINDOW)
            rk = slice(i * WINDOW, (i + 2) * WINDOW)
            s = s_next
            if n + 1 < len(work):
                s_next = scores(*work[n + 1])
            probs = [[None, None], [None, None]]
            for g in range(GROUP):
                p, c = divmod(g, 2)
                h = kv * GROUP + g
                sink = sink_ref[h]
                sg = s[c][p * WINDOW:(p + 1) * WINDOW, :] + bias[h]
                if i == 0:
                    sg = sg + no_prev
                m = jnp.maximum(jnp.max(sg, axis=-1, keepdims=True), sink)
                e = jnp.exp(sg - m)
                pr = e / (jnp.sum(e, axis=-1, keepdims=True) + jnp.exp(sink - m))
                probs[c][p] = pr.astype(BF16)
            pv = (_dot(jnp.concatenate(probs[0], axis=0), v2[kv, 0, rk, :])
                  + _dot(jnp.concatenate(probs[1], axis=0), v2[kv, 1, rk, :]))
            for p in range(GROUP // 2):
                obuf[rq, kv * KV_DIM + p * 2 * HEAD_DIM:kv * KV_DIM + (p + 1) * 2 * HEAD_DIM] = (
                    pv[p * WINDOW:(p + 1) * WINDOW, :].astype(BF16))

    k2[:, :, 0:WINDOW, :] = k2[:, :, tm:tm + WINDOW, :]
    v2[:, :, 0:WINDOW, :] = v2[:, :, tm:tm + WINDOW, :]
    o_ref[0] = x + _dot(obuf[...], wout_ref[...])


def _swa_prompt(layer, x, sinks, slopes, g, wqkv, qg4, kg4, ones, wout):
    n, t, _ = x.shape
    tm = SWA_TM
    smem = pl.BlockSpec(memory_space=pltpu.SMEM)
    return pl.pallas_call(
        _swa_prompt_kernel,
        grid=(n, t // tm),
        in_specs=[
            smem,
            smem,
            pl.BlockSpec((1, tm, D_MODEL), lambda b, j: (b, j, 0)),
            _const_spec((1, D_MODEL)),
            _layer_spec((D_MODEL, QKV_DIM), layer),
            _const_spec((1, KV_DIM)),
            _const_spec((1, KV_DIM)),
            _const_spec((KV_DIM, KV_DIM)),
            _layer_spec((Q_DIM, D_MODEL), layer),
        ],
        out_specs=[
            pl.BlockSpec((1, tm, D_MODEL), lambda b, j: (b, j, 0)),
            pl.BlockSpec((1, WINDOW, KV_DIM), lambda b, j: (b, 0, 0)),
            pl.BlockSpec((1, WINDOW, KV_DIM), lambda b, j: (b, 0, 0)),
        ],
        out_shape=[
            jax.ShapeDtypeStruct((n, t, D_MODEL), F32),
            jax.ShapeDtypeStruct((n, WINDOW, KV_DIM), F32),
            jax.ShapeDtypeStruct((n, WINDOW, KV_DIM), F32),
        ],
        scratch_shapes=[
            pltpu.VMEM((N_KV, 2, WINDOW + tm, 2 * HEAD_DIM), BF16),
            pltpu.VMEM((N_KV, 2, WINDOW + tm, 2 * HEAD_DIM), BF16),
            pltpu.VMEM((tm, Q_DIM), BF16),
            pltpu.VMEM((tm, Q_DIM), BF16),
            pltpu.VMEM((N_HEADS, WINDOW, 2 * WINDOW), F32),
        ],
        compiler_params=_params(2),
        name="swa_prompt",
    )(sinks, slopes, x, g, wqkv, qg4, kg4, ones, wout)


def _swa_sample_qkv_kernel(x_ref, g_ref, wqkv_ref, wkvt_ref, qg_ref, kg_ref, kgt_ref,
                           q_ref, k_ref, v_ref, kt_ref, vt_ref):
    xb = _rms(x_ref[...], g_ref[...]).astype(BF16)
    qkv = _dot(xb, wqkv_ref[...])
    for h in range(N_HEADS):
        cs = slice(h * HEAD_DIM, (h + 1) * HEAD_DIM)
        q_ref[:, cs] = _head_norm(qkv[:, cs], qg_ref[...])
    for kv in range(N_KV):
        cs = slice(kv * HEAD_DIM, (kv + 1) * HEAD_DIM)
        k_ref[:, cs] = _head_norm(qkv[:, Q_DIM + kv * HEAD_DIM:Q_DIM + (kv + 1) * HEAD_DIM], kg_ref[...])
    v_ref[...] = qkv[:, Q_DIM + KV_DIM:]
    kvt = _dot_nt(wkvt_ref[...], xb)
    for kv in range(N_KV):
        rs = slice(kv * HEAD_DIM, (kv + 1) * HEAD_DIM)
        z = kvt[rs, :]
        kt_ref[rs, :] = z * lax.rsqrt(jnp.mean(z * z, axis=0, keepdims=True) + RMS_EPS) * kgt_ref[...]
    vt_ref[...] = kvt[KV_DIM:, :]


def _swa_sample_qkv(layer, x, g, wqkv, wkvt, qg, kg, kgt):
    b = x.shape[0]
    stacked = {2: (D_MODEL, QKV_DIM), 3: (2 * KV_DIM, D_MODEL)}
    shapes = [(b, D_MODEL), (1, D_MODEL), None, None, (1, HEAD_DIM), (1, HEAD_DIM), (HEAD_DIM, b)]
    out_shapes = [(b, Q_DIM), (b, KV_DIM), (b, KV_DIM), (KV_DIM, b), (KV_DIM, b)]
    return pl.pallas_call(
        _swa_sample_qkv_kernel,
        grid=(1,),
        in_specs=[_layer_spec(stacked[k], layer) if s is None else _const_spec(s) for k, s in enumerate(shapes)],
        out_specs=[pl.BlockSpec(s, lambda i: (0, 0)) for s in out_shapes],
        out_shape=[jax.ShapeDtypeStruct(s, F32) for s in out_shapes],
        compiler_params=_params(1),
        name="swa_sample_qkv",
    )(x, g, wqkv, wkvt, qg, kg, kgt)


def _swa_sample_attn_kernel(q_ref, kn_ref, vn_ref, knt_ref, vnt_ref, kc_ref, vc_ref, sink_ref, slope_ref,
                            o_ref, ko_ref, vo_ref):
    bs = q_ref.shape[0]
    w = kc_ref.shape[3]
    i = pl.program_id(0)
    hrow = lax.broadcasted_iota(jnp.int32, (N_HEADS, KV_DIM), 0)
    feat = lax.broadcasted_iota(jnp.int32, (N_HEADS, KV_DIM), 1)
    own = (hrow // GROUP) == (feat // HEAD_DIM)
    key = lax.broadcasted_iota(jnp.int32, (N_HEADS, w), 1)
    dist = w - key
    bias = jnp.where(dist < WINDOW, -(slope_ref[:, 0:1] * dist.astype(F32)), NEG_INF)
    sink = sink_ref[:, 0:1]
    lane = lax.broadcasted_iota(jnp.int32, (KV_DIM, w), 1)
    newest = lane == w - 1

    def scores(b):
        q = q_ref[b]
        qt = jnp.concatenate([q] * N_KV, axis=1)
        qm = (jnp.where(own, qt, 0.0) * (HEAD_DIM ** -0.5)).astype(BF16)
        return qm, _dot(qm, kc_ref[0, b].astype(BF16)) + bias

    def softmax(b, qm, s):
        kn = kn_ref[b].astype(BF16).astype(F32)
        s_new = jnp.sum(qm.astype(F32) * kn, axis=-1, keepdims=True)
        m = jnp.maximum(jnp.maximum(jnp.max(s, axis=-1, keepdims=True), s_new), sink)
        e = jnp.exp(s - m)
        e_new = jnp.exp(s_new - m)
        den = jnp.sum(e, axis=-1, keepdims=True) + e_new + jnp.exp(sink - m)
        return (e / den).astype(BF16), (e_new / den).astype(BF16).astype(F32)

    def values(b, p, p_new):
        vn = vn_ref[b].astype(BF16).astype(F32)
        return _dot_nt(p, vc_ref[0, b].astype(BF16)) + p_new * vn

    def emit(b, res):
        res = jnp.where(own, res, 0.0)
        o = res[:, 0:HEAD_DIM]
        for kv in range(1, N_KV):
            o = o + res[:, kv * HEAD_DIM:(kv + 1) * HEAD_DIM]
        o_ref[b] = o

    def shift_in(b):
        mine = lane == i * bs + b
        kcol = jnp.sum(jnp.where(mine, knt_ref[...], 0.0), axis=1, keepdims=True)
        vcol = jnp.sum(jnp.where(mine, vnt_ref[...], 0.0), axis=1, keepdims=True)
        ko_ref[b] = jnp.where(newest, kcol, pltpu.roll(kc_ref[0, b], w - 1, 1))
        vo_ref[b] = jnp.where(newest, vcol, pltpu.roll(vc_ref[0, b], w - 1, 1))

    def body(grp, carry):
        seqs = [grp * SWA_SAMPLE_UNROLL + u for u in range(SWA_SAMPLE_UNROLL)]
        staged = [scores(b) for b in seqs]
        probs = [softmax(b, qm, s) for b, (qm, s) in zip(seqs, staged)]
        for b in seqs:
            shift_in(b)
        outs = [values(b, p, p_new) for b, (p, p_new) in zip(seqs, probs)]
        for b, res in zip(seqs, outs):
            emit(b, res)
        return carry

    lax.fori_loop(0, bs // SWA_SAMPLE_UNROLL, body, 0)


def _swa_sample_attn(layer, q3, kn3, vn3, knt, vnt, kct, vct, sink_b, slope_b):
    _, b, _, w = kct.shape
    assert w == 128 and b == 128, "one lane tile of key positions; new-token columns indexed by lane"
    bs = SWA_SAMPLE_BS
    cache_spec = pl.BlockSpec((1, bs, KV_DIM, w), lambda i: (layer, i, 0, 0))
    return pl.pallas_call(
        _swa_sample_attn_kernel,
        grid=(b // bs,),
        in_specs=[
            pl.BlockSpec((bs, N_HEADS, HEAD_DIM), lambda i: (i, 0, 0)),
            pl.BlockSpec((bs, 1, KV_DIM), lambda i: (i, 0, 0)),
            pl.BlockSpec((bs, 1, KV_DIM), lambda i: (i, 0, 0)),
            _const_spec((KV_DIM, b)),
            _const_spec((KV_DIM, b)),
            cache_spec,
            cache_spec,
            _const_spec((N_HEADS, 128)),
            _const_spec((N_HEADS, 128)),
        ],
        out_specs=[
            pl.BlockSpec((bs, N_HEADS, HEAD_DIM), lambda i: (i, 0, 0)),
            pl.BlockSpec((bs, KV_DIM, w), lambda i: (i, 0, 0)),
            pl.BlockSpec((bs, KV_DIM, w), lambda i: (i, 0, 0)),
        ],
        out_shape=[
            jax.ShapeDtypeStruct((b, N_HEADS, HEAD_DIM), F32),
            jax.ShapeDtypeStruct((b, KV_DIM, w), F32),
            jax.ShapeDtypeStruct((b, KV_DIM, w), F32),
        ],
        compiler_params=_params(1),
        name="swa_sample_attn",
    )(q3, kn3, vn3, knt, vnt, kct, vct, sink_b, slope_b)


def _proj_residual_kernel(x_ref, o_ref_in, wout_ref, y_ref):
    y_ref[...] = x_ref[...] + _dot(o_ref_in[...].astype(BF16), wout_ref[...])


def _proj_residual(layer, x, o, wout):
    b = x.shape[0]
    return pl.pallas_call(
        _proj_residual_kernel,
        grid=(1,),
        in_specs=[_const_spec((b, D_MODEL)), _const_spec((b, Q_DIM)), _layer_spec((Q_DIM, D_MODEL), layer)],
        out_specs=pl.BlockSpec((b, D_MODEL), lambda i: (0, 0)),
        out_shape=jax.ShapeDtypeStruct((b, D_MODEL), F32),
        compiler_params=_params(1),
        name="proj_residual",
    )(x, o, wout)


def kernel(x_prompt, x_sample, state_rglru_h, state_rglru_conv, cache_swa_k, cache_swa_v, norm_mix_g, norm_mlp_g, lru_w_in, lru_conv_w, lru_conv_b, lru_w_a, lru_b_a, lru_w_x, lru_b_x, lru_lambda, lru_w_out, attn_w_qkv, attn_q_norm, attn_k_norm, attn_sinks, attn_w_out, mlp_w_up, mlp_w_down):
    n_p, t_p, _ = x_prompt.shape
    n_s = x_sample.shape[0]
    w_buf = cache_swa_k.shape[2]
    heads = jnp.arange(1, N_HEADS + 1, dtype=F32)
    slopes = jnp.exp2(-8.0 * heads / N_HEADS)
    slopes_b = jnp.broadcast_to(slopes[:, None], (N_HEADS, 128))
    head_of_lane = jnp.arange(KV_DIM) // HEAD_DIM
    head_ones = (head_of_lane[:, None] == head_of_lane[None, :]).astype(BF16)

    to_feature_major = lambda c: c.transpose(0, 1, 3, 4, 2).reshape(c.shape[0], n_s, KV_DIM, w_buf)
    from_feature_major = lambda c: c.reshape(c.shape[0], n_s, N_KV, HEAD_DIM, w_buf).transpose(0, 1, 4, 2, 3)
    kct, vct = to_feature_major(cache_swa_k), to_feature_major(cache_swa_v)

    win = lru_w_in.astype(BF16)
    wax = jnp.concatenate([lru_w_a, lru_w_x], axis=-1).astype(BF16)
    wlo = lru_w_out.astype(BF16)
    wqkv = attn_w_qkv.astype(BF16)
    wkvt = attn_w_qkv[:, :, Q_DIM:].transpose(0, 2, 1).astype(BF16)
    wao = attn_w_out.astype(BF16)
    wu = mlp_w_up.astype(BF16)
    wd = mlp_w_down.astype(BF16)

    row = lambda v: v.reshape(1, -1)
    yp = x_prompt
    ys = x_sample.reshape(n_s, D_MODEL)
    h_p, c_p, k_p, v_p, h_s, c_s, k_s, v_s = ([] for _ in range(8))
    for layer in range(DEPTH):
        j = layer // 2
        g_mix = row(norm_mix_g[layer])
        if layer % 2 == 0:
            args = (g_mix, win, lru_conv_w[j], row(lru_conv_b[j]), wax, row(lru_b_a[j]), row(lru_b_x[j]),
                    row(lru_lambda[j]), wlo)
            ys, hs, cs = _lru_sample(j, ys, state_rglru_h[j], state_rglru_conv[j].transpose(1, 0, 2), *args)
            yp, ys, hp, cp = _lru_mlp(j, layer, yp, ys, *args, row(norm_mlp_g[layer]), wu, wd)
            h_p.append(hp.reshape(n_p, D_RNN)); c_p.append(cp)
            h_s.append(hs); c_s.append(cs.transpose(1, 0, 2))
            continue
        else:
            qg, kg = row(attn_q_norm[j]), row(attn_k_norm[j])
            sinks = attn_sinks[j]
            yp, kp, vp = _swa_prompt(j, yp, sinks, slopes, g_mix, wqkv, jnp.tile(qg, (1, GROUP)),
                                     jnp.tile(kg, (1, N_KV)), head_ones, wao)
            k_p.append(kp.reshape(n_p, w_buf, N_KV, HEAD_DIM)); v_p.append(vp.reshape(n_p, w_buf, N_KV, HEAD_DIM))
            kgt = jnp.broadcast_to(attn_k_norm[j][:, None], (HEAD_DIM, n_s))
            q, kn, vn, knt, vnt = _swa_sample_qkv(j, ys, g_mix, wqkv, wkvt, qg, kg, kgt)
            o3, ks, vs = _swa_sample_attn(
                j, q.reshape(n_s, N_HEADS, HEAD_DIM), kn.reshape(n_s, 1, KV_DIM), vn.reshape(n_s, 1, KV_DIM),
                knt, vnt, kct, vct, jnp.broadcast_to(sinks[:, None], (N_HEADS, 128)), slopes_b)
            k_s.append(ks); v_s.append(vs)
            ys = _proj_residual(j, ys, o3.reshape(n_s, Q_DIM), wao)
        yp, ys = _mlp(yp.reshape(n_p * t_p, D_MODEL), ys, row(norm_mlp_g[layer]), wu, wd, layer)
        yp = yp.reshape(n_p, t_p, D_MODEL)
    return (yp, ys.reshape(n_s, 1, D_MODEL),
            jnp.stack(h_p), jnp.stack(c_p), jnp.stack(k_p), jnp.stack(v_p),
            jnp.stack(h_s), jnp.stack(c_s), from_feature_major(jnp.stack(k_s)), from_feature_major(jnp.stack(v_s)))
```

```python
import functools

import jax
import jax.numpy as jnp
from jax import lax
from jax.experimental import pallas as pl
from jax.experimental.pallas import tpu as pltpu

F32 = jnp.float32
BF16 = jnp.bfloat16

D_MODEL = 1024
D_RNN = 1024
D_FF = 4096
DEPTH = 4
CONV_W = 4
LRU_BLOCKS = 4
LRU_BW = D_RNN // LRU_BLOCKS
LRU_C = 8.0
N_HEADS = 16
N_KV = 4
GROUP = N_HEADS // N_KV
HEAD_DIM = 64
Q_DIM = N_HEADS * HEAD_DIM
KV_DIM = N_KV * HEAD_DIM
QKV_DIM = Q_DIM + 2 * KV_DIM
WINDOW = 128
assert GROUP * HEAD_DIM == KV_DIM and N_KV % 2 == 0 and GROUP % 2 == 0
RMS_EPS = 1e-6
NEG_INF = -1e30

V7X_VMEM_LIMIT_BYTES = 56 * 1024 * 1024
SUBLANES = 8

MLP_FC = 1024
LRU_TM = 256
SWA_TM = 256
SWA_SAMPLE_BS = 16
SWA_SAMPLE_UNROLL = 8


def _params(n_axes):
    return pltpu.CompilerParams(
        dimension_semantics=("arbitrary",) * n_axes,
        vmem_limit_bytes=V7X_VMEM_LIMIT_BYTES,
    )


def _const_spec(shape):
    nd = len(shape)
    return pl.BlockSpec(shape, lambda *_: (0,) * nd, pipeline_mode=pl.Buffered(1))


def _rms(x, g):
    return x * lax.rsqrt(jnp.mean(x * x, axis=-1, keepdims=True) + RMS_EPS) * g


def _dot(a, b):
    return jnp.dot(a, b, preferred_element_type=F32)


def _dot_nt(a, b):
    return lax.dot_general(a, b, (((1,), (1,)), ((), ())), preferred_element_type=F32)


def _layer_spec(shape, layer):
    nd = len(shape)
    return pl.BlockSpec((None,) + shape, lambda *_: (layer,) + (0,) * nd, pipeline_mode=pl.Buffered(1))


def _softplus(x):
    return jnp.maximum(x, 0.0) + jnp.log1p(jnp.exp(-jnp.abs(x)))


def _lru_gates(xc_k, ga, ba_k, bx_k, sp_k):
    r = jax.nn.sigmoid(ga[:, :LRU_BW] + ba_k)
    i = jax.nn.sigmoid(ga[:, LRU_BW:] + bx_k)
    log_a = -LRU_C * r * sp_k
    a = jnp.exp(log_a)
    u = jnp.sqrt(-jnp.tanh(log_a) * (a * a + 1.0)) * (i * xc_k)
    return a, u


def _scan_rows(a, u, h0):
    rows, c = a.shape
    groups = rows // SUBLANES
    a = a.reshape(groups, SUBLANES, c)
    u = u.reshape(groups, SUBLANES, c)
    row = lax.broadcasted_iota(jnp.int32, a.shape, 1)
    d = 1
    while d < SUBLANES:
        keep = row >= d
        u = jnp.where(keep, u + a * pltpu.roll(u, d, 1), u)
        a = jnp.where(keep, a * pltpu.roll(a, d, 1), a)
        d *= 2
    hs = []
    h = h0
    for g in range(groups):
        hg = a[g] * h + u[g]
        hs.append(hg)
        h = hg[SUBLANES - 1:SUBLANES, :]
    return jnp.concatenate(hs, axis=0)


def _lru_mlp_kernel(n_seq, x_ref, xs_ref, g_ref, win_ref, cw_ref, cb_ref, wax_ref, ba_ref, bx_ref, lam_ref,
                    wout_ref, g2_ref, wu_ref, wd_ref, o_ref, os_ref, h_ref, c_ref, xbuf, hcar, ymid):
    tm = x_ref.shape[1]
    nj = (pl.num_programs(0) - 1) // n_seq - 1
    i = pl.program_id(0)
    j = lax.rem(i, nj + 1)

    def mlp_chunk(xb, c):
        h = _dot(xb, wu_ref[:, c * MLP_FC:(c + 1) * MLP_FC])
        return jnp.square(jnp.maximum(h, 0.0)).astype(BF16)

    def step(do_mixer, do_mlp):
        if do_mlp:
            yprev = ymid[...]
            yb = _rms(yprev, g2_ref[...]).astype(BF16)
            mlp = yprev
        if not do_mixer:
            for c in range(D_FF // MLP_FC):
                mlp = mlp + _dot(mlp_chunk(yb, c), wd_ref[c * MLP_FC:(c + 1) * MLP_FC, :])
            o_ref[0] = mlp
            return
        x = x_ref[0]
        xb = _rms(x, g_ref[...]).astype(BF16)
        u_in = _dot(xb, win_ref[...])
        xr = u_in[:, D_RNN:]
        xbuf[SUBLANES:SUBLANES + tm, :] = xr
        cw = cw_ref[...]
        xc = xbuf[pl.ds(SUBLANES - 3, tm), :] * cw[0:1, :]
        xc = xc + xbuf[pl.ds(SUBLANES - 2, tm), :] * cw[1:2, :]
        xc = xc + xbuf[pl.ds(SUBLANES - 1, tm), :] * cw[2:3, :]
        xc = xc + xr * cw[3:4, :]
        xc = xc + cb_ref[...]
        xbuf[0:SUBLANES, :] = xbuf[tm:tm + SUBLANES, :]

        sp = _softplus(-lam_ref[...])
        blocks = [slice(k * LRU_BW, (k + 1) * LRU_BW) for k in range(LRU_BLOCKS)]
        gates = [_dot(xc[:, cs].astype(BF16), wax_ref[k]) for k, cs in enumerate(blocks)]
        acc = x
        for k, cs in enumerate(blocks):
            if do_mlp:
                hk = mlp_chunk(yb, k)
            a, u = _lru_gates(xc[:, cs], gates[k], ba_ref[:, cs], bx_ref[:, cs], sp[:, cs])
            h = _scan_rows(a, u, hcar[:, cs])
            hcar[:, cs] = h[tm - 1:tm, :]
            y = h * jax.nn.gelu(u_in[:, cs])
            if do_mlp:
                mlp = mlp + _dot(hk, wd_ref[k * MLP_FC:(k + 1) * MLP_FC, :])
            acc = acc + _dot(y.astype(BF16), wout_ref[cs, :])
        ymid[...] = acc
        if do_mlp:
            o_ref[0] = mlp

    prompt = i < n_seq * (nj + 1)

    @pl.when(prompt & (j == 0))
    def _():
        xbuf[0:SUBLANES, :] = jnp.zeros((SUBLANES, D_RNN), F32)
        hcar[...] = jnp.zeros_like(hcar)
        step(True, False)

    pl.when(prompt & (j > 0) & (j < nj))(lambda: step(True, True))
    pl.when(prompt & (j == nj))(lambda: step(False, True))

    @pl.when(prompt & (j == nj - 1))
    def _():
        h_ref[0] = hcar[...]
        c_ref[0] = xbuf[pl.ds(SUBLANES - (CONV_W - 1), CONV_W - 1), :]

    @pl.when(i == n_seq * (nj + 1))
    def _():
        x = xs_ref[...]
        xb = _rms(x, g2_ref[...]).astype(BF16)
        acc = x
        for c in range(D_FF // MLP_FC):
            acc = acc + _dot(mlp_chunk(xb, c), wd_ref[c * MLP_FC:(c + 1) * MLP_FC, :])
        os_ref[...] = acc


def _lru_mlp(layer, mlp_layer, x, xs, g, win, cw, cb, wax, ba, bx, lam, wout, g2, wu, wd):
    n, t, _ = x.shape
    rows_s = xs.shape[0]
    tm = LRU_TM
    assert D_FF // MLP_FC == LRU_BLOCKS
    nj = t // tm
    steps = nj + 1
    seq = lambda i: jnp.minimum(i // steps, n - 1)
    chunk_in = lambda i: jnp.where(i >= n * steps, nj - 1, jnp.minimum(lax.rem(i, steps), nj - 1))
    chunk_out = lambda i: jnp.where(i >= n * steps, nj - 1, jnp.maximum(lax.rem(i, steps) - 1, 0))
    sample_spec = pl.BlockSpec((rows_s, D_MODEL), lambda i: (0, 0))
    return pl.pallas_call(
        functools.partial(_lru_mlp_kernel, n),
        grid=(n * steps + 1,),
        in_specs=[
            pl.BlockSpec((1, tm, D_MODEL), lambda i: (seq(i), chunk_in(i), 0)),
            sample_spec,
            _const_spec((1, D_MODEL)),
            _layer_spec((D_MODEL, 2 * D_RNN), layer),
            _const_spec((CONV_W, D_RNN)),
            _const_spec((1, D_RNN)),
            _layer_spec((LRU_BLOCKS, LRU_BW, 2 * LRU_BW), layer),
            _const_spec((1, D_RNN)),
            _const_spec((1, D_RNN)),
            _const_spec((1, D_RNN)),
            _layer_spec((D_RNN, D_MODEL), layer),
            _const_spec((1, D_MODEL)),
            _layer_spec((D_MODEL, D_FF), mlp_layer),
            _layer_spec((D_FF, D_MODEL), mlp_layer),
        ],
        out_specs=[
            pl.BlockSpec((1, tm, D_MODEL), lambda i: (seq(i), chunk_out(i), 0)),
            sample_spec,
            pl.BlockSpec((1, 1, D_RNN), lambda i: (seq(i), 0, 0)),
            pl.BlockSpec((1, CONV_W - 1, D_RNN), lambda i: (seq(i), 0, 0)),
        ],
        out_shape=[
            jax.ShapeDtypeStruct((n, t, D_MODEL), F32),
            jax.ShapeDtypeStruct((rows_s, D_MODEL), F32),
            jax.ShapeDtypeStruct((n, 1, D_RNN), F32),
            jax.ShapeDtypeStruct((n, CONV_W - 1, D_RNN), F32),
        ],
        scratch_shapes=[
            pltpu.VMEM((SUBLANES + tm, D_RNN), F32),
            pltpu.VMEM((1, D_RNN), F32),
            pltpu.VMEM((tm, D_MODEL), F32),
        ],
        compiler_params=_params(1),
        name="lru_mlp",
    )(x, xs, g, win, cw, cb, wax, ba, bx, lam, wout, g2, wu, wd)


def _lru_sample_kernel(x_ref, h0_ref, c0_ref, g_ref, win_ref, cw_ref, cb_ref, wax_ref, ba_ref, bx_ref, lam_ref,
                       wout_ref, o_ref, h_ref, c_ref):
    x = x_ref[...]
    xb = _rms(x, g_ref[...]).astype(BF16)
    u_in = _dot(xb, win_ref[...])
    xr = u_in[:, D_RNN:]
    cw = cw_ref[...]
    xc = c0_ref[0] * cw[0:1, :]
    xc = xc + c0_ref[1] * cw[1:2, :]
    xc = xc + c0_ref[2] * cw[2:3, :]
    xc = xc + xr * cw[3:4, :]
    xc = xc + cb_ref[...]
    c_ref[0] = c0_ref[1]
    c_ref[1] = c0_ref[2]
    c_ref[2] = xr

    sp = _softplus(-lam_ref[...])
    acc = x
    for k in range(LRU_BLOCKS):
        cs = slice(k * LRU_BW, (k + 1) * LRU_BW)
        ga = _dot(xc[:, cs].astype(BF16), wax_ref[k])
        a, u = _lru_gates(xc[:, cs], ga, ba_ref[:, cs], bx_ref[:, cs], sp[:, cs])
        h = a * h0_ref[:, cs] + u
        h_ref[:, cs] = h
        y = h * jax.nn.gelu(u_in[:, cs])
        acc = acc + _dot(y.astype(BF16), wout_ref[cs, :])
    o_ref[...] = acc


def _lru_sample(layer, x, h0, c0, g, win, cw, cb, wax, ba, bx, lam, wout):
    b = x.shape[0]
    stacked = {4: (D_MODEL, 2 * D_RNN), 7: (LRU_BLOCKS, LRU_BW, 2 * LRU_BW), 11: (D_RNN, D_MODEL)}
    shapes = [(b, D_MODEL), (b, D_RNN), (CONV_W - 1, b, D_RNN), (1, D_MODEL), None,
              (CONV_W, D_RNN), (1, D_RNN), None, (1, D_RNN), (1, D_RNN), (1, D_RNN), None]
    return pl.pallas_call(
        _lru_sample_kernel,
        grid=(1,),
        in_specs=[_layer_spec(stacked[k], layer) if s is None else _const_spec(s) for k, s in enumerate(shapes)],
        out_specs=[
            pl.BlockSpec((b, D_MODEL), lambda i: (0, 0)),
            pl.BlockSpec((b, D_RNN), lambda i: (0, 0)),
            pl.BlockSpec((CONV_W - 1, b, D_RNN), lambda i: (0, 0, 0)),
        ],
        out_shape=[
            jax.ShapeDtypeStruct((b, D_MODEL), F32),
            jax.ShapeDtypeStruct((b, D_RNN), F32),
            jax.ShapeDtypeStruct((CONV_W - 1, b, D_RNN), F32),
        ],
        compiler_params=_params(1),
        name="lru_sample",
    )(x, h0, c0, g, win, cw, cb, wax, ba, bx, lam, wout)


def _head_norm(z, g):
    return z * lax.rsqrt(jnp.mean(z * z, axis=-1, keepdims=True) + RMS_EPS) * g


def _heads_norm(z, ones, g):
    z2 = z * z
    hi = z2.astype(BF16)
    lo = (z2 - hi.astype(F32)).astype(BF16)
    ssq = _dot(hi, ones) + _dot(lo, ones)
    return z * lax.rsqrt(ssq * (1.0 / HEAD_DIM) + RMS_EPS) * g


_MLP_ORDER = (("up", 0), ("up", 1), ("down", 0), ("up", 2), ("down", 1), ("up", 3), ("down", 2), ("down", 3))


def _swa_mlp_kernel(n_seq, sink_ref, slope_ref, x_ref, xs_ref, g_ref, wqkv_ref, qg_ref, kg_ref, ones_ref,
                    wout_ref, g2_ref, wu_ref, wd_ref, o_ref, os_ref, k_ref, v_ref, k2, v2, qn, obuf, bias, ymid):
    i = pl.program_id(0)
    nj = (pl.num_programs(0) - 1) // n_seq - 1

    def mlp_up(xb, c):
        h = _dot(xb, wu_ref[:, c * MLP_FC:(c + 1) * MLP_FC])
        return jnp.square(jnp.maximum(h, 0.0)).astype(BF16)

    def mlp_down(hk, c):
        return _dot(hk, wd_ref[c * MLP_FC:(c + 1) * MLP_FC, :])

    step = functools.partial(_swa_mlp_step, i == 0, sink_ref, slope_ref, x_ref, g_ref, wqkv_ref, qg_ref, kg_ref,
                             ones_ref, wout_ref, g2_ref, mlp_up, mlp_down, o_ref, k_ref, v_ref, k2, v2, qn, obuf,
                             bias, ymid)
    prompt = i < n_seq * (nj + 1)
    j = lax.rem(i, nj + 1)
    pl.when(prompt & (j == 0))(lambda: step(True, False))
    pl.when(prompt & (j > 0) & (j < nj))(lambda: step(True, True))
    pl.when(prompt & (j == nj))(lambda: step(False, True))

    @pl.when(i == n_seq * (nj + 1))
    def _():
        x = xs_ref[...]
        xb = _rms(x, g2_ref[...]).astype(BF16)
        acc = x
        for c in range(D_FF // MLP_FC):
            acc = acc + mlp_down(mlp_up(xb, c), c)
        os_ref[...] = acc


def _swa_mlp_step(first, sink_ref, slope_ref, x_ref, g_ref, wqkv_ref, qg_ref, kg_ref, ones_ref, wout_ref,
                  g2_ref, mlp_up, mlp_down, o_ref, k_ref, v_ref, k2, v2, qn, obuf, bias, ymid, do_mixer, do_mlp):
    tm = x_ref.shape[1]
    nblk = tm // WINDOW
    first_chunk = not do_mlp

    if do_mlp:
        yprev = ymid[...]
        yb = _rms(yprev, g2_ref[...]).astype(BF16)
        mlp = yprev
    if not do_mixer:
        for c in range(D_FF // MLP_FC):
            mlp = mlp + mlp_down(mlp_up(yb, c), c)
        o_ref[0] = mlp
        return

    if first_chunk:
        @pl.when(first)
        def _():
            qi = lax.broadcasted_iota(jnp.int32, (WINDOW, 2 * WINDOW), 0)
            si = lax.broadcasted_iota(jnp.int32, (WINDOW, 2 * WINDOW), 1)
            dist = qi + WINDOW - si
            valid = (dist >= 0) & (dist < WINDOW)
            dist_f = dist.astype(F32)
            for h in range(N_HEADS):
                bias[h] = jnp.where(valid, -(slope_ref[h] * dist_f), NEG_INF)

        k2[:, :, 0:WINDOW, :] = jnp.zeros((N_KV, 2, WINDOW, 2 * HEAD_DIM), BF16)
        v2[:, :, 0:WINDOW, :] = jnp.zeros((N_KV, 2, WINDOW, 2 * HEAD_DIM), BF16)

    x = x_ref[0]
    xb = _rms(x, g_ref[...]).astype(BF16)
    qkv = _dot(xb, wqkv_ref[...])

    ones = ones_ref[...]
    kn = _heads_norm(qkv[:, Q_DIM:Q_DIM + KV_DIM], ones, kg_ref[...])
    vv = qkv[:, Q_DIM + KV_DIM:]
    k_ref[0] = kn[tm - WINDOW:, :]
    v_ref[0] = vv[tm - WINDOW:, :]
    low = lax.broadcasted_iota(jnp.int32, (tm, 2 * HEAD_DIM), 1) < HEAD_DIM
    for src, dst in ((kn, k2), (vv, v2)):
        for sl in range(N_KV // 2):
            z = src[:, sl * 2 * HEAD_DIM:(sl + 1) * 2 * HEAD_DIM]
            zs = pltpu.roll(z, HEAD_DIM, 1)
            dst[2 * sl, 0, WINDOW:WINDOW + tm, :] = jnp.where(low, z, 0.0).astype(BF16)
            dst[2 * sl, 1, WINDOW:WINDOW + tm, :] = jnp.where(low, 0.0, zs).astype(BF16)
            dst[2 * sl + 1, 0, WINDOW:WINDOW + tm, :] = jnp.where(low, zs, 0.0).astype(BF16)
            dst[2 * sl + 1, 1, WINDOW:WINDOW + tm, :] = jnp.where(low, 0.0, z).astype(BF16)
    qg = qg_ref[...] * (HEAD_DIM ** -0.5)
    for kv in range(N_KV):
        cs = slice(kv * KV_DIM, (kv + 1) * KV_DIM)
        qn[:, cs] = _heads_norm(qkv[:, cs], ones, qg).astype(BF16)

    col = lax.broadcasted_iota(jnp.int32, (1, 2 * WINDOW), 1)
    no_prev = jnp.where(col < WINDOW, NEG_INF, 0.0)

    def scores(kv, i):
        rq = slice(i * WINDOW, (i + 1) * WINDOW)
        ql = jnp.concatenate([qn[rq, kv * KV_DIM + p * 2 * HEAD_DIM:kv * KV_DIM + (p + 1) * 2 * HEAD_DIM]
                              for p in range(GROUP // 2)], axis=0)
        return [_dot_nt(ql, k2[kv, c, i * WINDOW:(i + 2) * WINDOW, :]) for c in range(2)]

    work = [(kv, i) for kv in range(N_KV) for i in range(nblk)]
    hidden = {}
    s_next = scores(*work[0])
    for n, (kv, i) in enumerate(work):
        mlp_ops = _MLP_ORDER[n * len(_MLP_ORDER) // len(work):(n + 1) * len(_MLP_ORDER) // len(work)]
        for kind, c in mlp_ops if do_mlp else ():
            if kind == "up":
                hidden[c] = mlp_up(yb, c)
            else:
                mlp = mlp + mlp_down(hidden.pop(c), c)
        if True:
            rq = slice(i * WINDOW, (i + 1) * WINDOW)
            rk = slice(i * WINDOW, (i + 2) * WINDOW)
            s = s_next
            if n + 1 < len(work):
                s_next = scores(*work[n + 1])
            probs = [[None, None], [None, None]]
            for g in range(GROUP):
                p, c = divmod(g, 2)
                h = kv * GROUP + g
                sink = sink_ref[h]
                sg = s[c][p * WINDOW:(p + 1) * WINDOW, :] + bias[h]
                if i == 0 and first_chunk:
                    sg = sg + no_prev
                m = jnp.maximum(jnp.max(sg, axis=-1, keepdims=True), sink)
                e = jnp.exp(sg - m)
                pr = e / (jnp.sum(e, axis=-1, keepdims=True) + jnp.exp(sink - m))
                probs[c][p] = pr.astype(BF16)
            pv = (_dot(jnp.concatenate(probs[0], axis=0), v2[kv, 0, rk, :])
                  + _dot(jnp.concatenate(probs[1], axis=0), v2[kv, 1, rk, :]))
            for p in range(GROUP // 2):
                obuf[rq, kv * KV_DIM + p * 2 * HEAD_DIM:kv * KV_DIM + (p + 1) * 2 * HEAD_DIM] = (
                    pv[p * WINDOW:(p + 1) * WINDOW, :].astype(BF16))

    assert not hidden
    k2[:, :, 0:WINDOW, :] = k2[:, :, tm:tm + WINDOW, :]
    v2[:, :, 0:WINDOW, :] = v2[:, :, tm:tm + WINDOW, :]
    ymid[...] = x + _dot(obuf[...], wout_ref[...])
    if do_mlp:
        o_ref[0] = mlp


def _swa_mlp(layer, mlp_layer, x, xs, sinks, slopes, g, wqkv, qg4, kg4, ones, wout, g2, wu, wd):
    n, t, _ = x.shape
    rows_s = xs.shape[0]
    tm = SWA_TM
    nj = t // tm
    steps = nj + 1
    seq = lambda i: jnp.minimum(i // steps, n - 1)
    chunk_in = lambda i: jnp.where(i >= n * steps, nj - 1, jnp.minimum(lax.rem(i, steps), nj - 1))
    chunk_out = lambda i: jnp.where(i >= n * steps, nj - 1, jnp.maximum(lax.rem(i, steps) - 1, 0))
    sample_spec = pl.BlockSpec((rows_s, D_MODEL), lambda i: (0, 0))
    smem = pl.BlockSpec(memory_space=pltpu.SMEM)
    return pl.pallas_call(
        functools.partial(_swa_mlp_kernel, n),
        grid=(n * steps + 1,),
        in_specs=[
            smem,
            smem,
            pl.BlockSpec((1, tm, D_MODEL), lambda i: (seq(i), chunk_in(i), 0)),
            sample_spec,
            _const_spec((1, D_MODEL)),
            _layer_spec((D_MODEL, QKV_DIM), layer),
            _const_spec((1, KV_DIM)),
            _const_spec((1, KV_DIM)),
            _const_spec((KV_DIM, KV_DIM)),
            _layer_spec((Q_DIM, D_MODEL), layer),
            _const_spec((1, D_MODEL)),
            _layer_spec((D_MODEL, D_FF), mlp_layer),
            _layer_spec((D_FF, D_MODEL), mlp_layer),
        ],
        out_specs=[
            pl.BlockSpec((1, tm, D_MODEL), lambda i: (seq(i), chunk_out(i), 0)),
            sample_spec,
            pl.BlockSpec((1, WINDOW, KV_DIM), lambda i: (seq(i), 0, 0)),
            pl.BlockSpec((1, WINDOW, KV_DIM), lambda i: (seq(i), 0, 0)),
        ],
        out_shape=[
            jax.ShapeDtypeStruct((n, t, D_MODEL), F32),
            jax.ShapeDtypeStruct((rows_s, D_MODEL), F32),
            jax.ShapeDtypeStruct((n, WINDOW, KV_DIM), F32),
            jax.ShapeDtypeStruct((n, WINDOW, KV_DIM), F32),
        ],
        scratch_shapes=[
            pltpu.VMEM((N_KV, 2, WINDOW + tm, 2 * HEAD_DIM), BF16),
            pltpu.VMEM((N_KV, 2, WINDOW + tm, 2 * HEAD_DIM), BF16),
            pltpu.VMEM((tm, Q_DIM), BF16),
            pltpu.VMEM((tm, Q_DIM), BF16),
            pltpu.VMEM((N_HEADS, WINDOW, 2 * WINDOW), F32),
            pltpu.VMEM((tm, D_MODEL), F32),
        ],
        compiler_params=_params(1),
        name="swa_mlp",
    )(sinks, slopes, x, xs, g, wqkv, qg4, kg4, ones, wout, g2, wu, wd)


def _swa_sample_qkv_kernel(x_ref, g_ref, wqkv_ref, wkvt_ref, qg_ref, kg_ref, kgt_ref,
                           q_ref, k_ref, v_ref, kt_ref, vt_ref):
    xb = _rms(x_ref[...], g_ref[...]).astype(BF16)
    qkv = _dot(xb, wqkv_ref[...])
    for h in range(N_HEADS):
        cs = slice(h * HEAD_DIM, (h + 1) * HEAD_DIM)
        q_ref[:, cs] = _head_norm(qkv[:, cs], qg_ref[...])
    for kv in range(N_KV):
        cs = slice(kv * HEAD_DIM, (kv + 1) * HEAD_DIM)
        k_ref[:, cs] = _head_norm(qkv[:, Q_DIM + kv * HEAD_DIM:Q_DIM + (kv + 1) * HEAD_DIM], kg_ref[...])
    v_ref[...] = qkv[:, Q_DIM + KV_DIM:]
    kvt = _dot_nt(wkvt_ref[...], xb)
    for kv in range(N_KV):
        rs = slice(kv * HEAD_DIM, (kv + 1) * HEAD_DIM)
        z = kvt[rs, :]
        kt_ref[rs, :] = z * lax.rsqrt(jnp.mean(z * z, axis=0, keepdims=True) + RMS_EPS) * kgt_ref[...]
    vt_ref[...] = kvt[KV_DIM:, :]


def _swa_sample_qkv(layer, x, g, wqkv, wkvt, qg, kg, kgt):
    b = x.shape[0]
    stacked = {2: (D_MODEL, QKV_DIM), 3: (2 * KV_DIM, D_MODEL)}
    shapes = [(b, D_MODEL), (1, D_MODEL), None, None, (1, HEAD_DIM), (1, HEAD_DIM), (HEAD_DIM, b)]
    out_shapes = [(b, Q_DIM), (b, KV_DIM), (b, KV_DIM), (KV_DIM, b), (KV_DIM, b)]
    return pl.pallas_call(
        _swa_sample_qkv_kernel,
        grid=(1,),
        in_specs=[_layer_spec(stacked[k], layer) if s is None else _const_spec(s) for k, s in enumerate(shapes)],
        out_specs=[pl.BlockSpec(s, lambda i: (0, 0)) for s in out_shapes],
        out_shape=[jax.ShapeDtypeStruct(s, F32) for s in out_shapes],
        compiler_params=_params(1),
        name="swa_sample_qkv",
    )(x, g, wqkv, wkvt, qg, kg, kgt)


def _swa_sample_attn_kernel(q_ref, kn_ref, vn_ref, knt_ref, vnt_ref, kc_ref, vc_ref, sink_ref, slope_ref,
                            o_ref, ko_ref, vo_ref):
    bs = q_ref.shape[0]
    w = kc_ref.shape[3]
    i = pl.program_id(0)
    hrow = lax.broadcasted_iota(jnp.int32, (N_HEADS, KV_DIM), 0)
    feat = lax.broadcasted_iota(jnp.int32, (N_HEADS, KV_DIM), 1)
    own = (hrow // GROUP) == (feat // HEAD_DIM)
    key = lax.broadcasted_iota(jnp.int32, (N_HEADS, w), 1)
    dist = w - key
    bias = jnp.where(dist < WINDOW, -(slope_ref[:, 0:1] * dist.astype(F32)), NEG_INF)
    sink = sink_ref[:, 0:1]
    lane = lax.broadcasted_iota(jnp.int32, (KV_DIM, w), 1)
    newest = lane == w - 1

    def scores(b):
        q = q_ref[b]
        qt = jnp.concatenate([q] * N_KV, axis=1)
        qm = (jnp.where(own, qt, 0.0) * (HEAD_DIM ** -0.5)).astype(BF16)
        return qm, _dot(qm, kc_ref[0, b].astype(BF16)) + bias

    def softmax(b, qm, s):
        kn = kn_ref[b].astype(BF16).astype(F32)
        s_new = jnp.sum(qm.astype(F32) * kn, axis=-1, keepdims=True)
        m = jnp.maximum(jnp.maximum(jnp.max(s, axis=-1, keepdims=True), s_new), sink)
        e = jnp.exp(s - m)
        e_new = jnp.exp(s_new - m)
        den = jnp.sum(e, axis=-1, keepdims=True) + e_new + jnp.exp(sink - m)
        return (e / den).astype(BF16), (e_new / den).astype(BF16).astype(F32)

    def values(b, p, p_new):
        vn = vn_ref[b].astype(BF16).astype(F32)
        return _dot_nt(p, vc_ref[0, b].astype(BF16)) + p_new * vn

    def emit(b, res):
        res = jnp.where(own, res, 0.0)
        o = res[:, 0:HEAD_DIM]
        for kv in range(1, N_KV):
            o = o + res[:, kv * HEAD_DIM:(kv + 1) * HEAD_DIM]
        o_ref[b] = o

    def shift_in(b):
        mine = lane == i * bs + b
        kcol = jnp.sum(jnp.where(mine, knt_ref[...], 0.0), axis=1, keepdims=True)
        vcol = jnp.sum(jnp.where(mine, vnt_ref[...], 0.0), axis=1, keepdims=True)
        ko_ref[b] = jnp.where(newest, kcol, pltpu.roll(kc_ref[0, b], w - 1, 1))
        vo_ref[b] = jnp.where(newest, vcol, pltpu.roll(vc_ref[0, b], w - 1, 1))

    def body(grp, carry):
        seqs = [grp * SWA_SAMPLE_UNROLL + u for u in range(SWA_SAMPLE_UNROLL)]
        staged = [scores(b) for b in seqs]
        probs = [softmax(b, qm, s) for b, (qm, s) in zip(seqs, staged)]
        for b in seqs:
            shift_in(b)
        outs = [values(b, p, p_new) for b, (p, p_new) in zip(seqs, probs)]
        for b, res in zip(seqs, outs):
            emit(b, res)
        return carry

    lax.fori_loop(0, bs // SWA_SAMPLE_UNROLL, body, 0)


def _swa_sample_attn(layer, q3, kn3, vn3, knt, vnt, kct, vct, sink_b, slope_b):
    _, b, _, w = kct.shape
    assert w == 128 and b == 128, "one lane tile of key positions; new-token columns indexed by lane"
    bs = SWA_SAMPLE_BS
    cache_spec = pl.BlockSpec((1, bs, KV_DIM, w), lambda i: (layer, i, 0, 0))
    return pl.pallas_call(
        _swa_sample_attn_kernel,
        grid=(b // bs,),
        in_specs=[
            pl.BlockSpec((bs, N_HEADS, HEAD_DIM), lambda i: (i, 0, 0)),
            pl.BlockSpec((bs, 1, KV_DIM), lambda i: (i, 0, 0)),
            pl.BlockSpec((bs, 1, KV_DIM), lambda i: (i, 0, 0)),
            _const_spec((KV_DIM, b)),
            _const_spec((KV_DIM, b)),
            cache_spec,
            cache_spec,
            _const_spec((N_HEADS, 128)),
            _const_spec((N_HEADS, 128)),
        ],
        out_specs=[
            pl.BlockSpec((bs, N_HEADS, HEAD_DIM), lambda i: (i, 0, 0)),
            pl.BlockSpec((bs, KV_DIM, w), lambda i: (i, 0, 0)),
            pl.BlockSpec((bs, KV_DIM, w), lambda i: (i, 0, 0)),
        ],
        out_shape=[
            jax.ShapeDtypeStruct((b, N_HEADS, HEAD_DIM), F32),
            jax.ShapeDtypeStruct((b, KV_DIM, w), F32),
            jax.ShapeDtypeStruct((b, KV_DIM, w), F32),
        ],
        compiler_params=_params(1),
        name="swa_sample_attn",
    )(q3, kn3, vn3, knt, vnt, kct, vct, sink_b, slope_b)


def _proj_residual_kernel(x_ref, o_ref_in, wout_ref, y_ref):
    y_ref[...] = x_ref[...] + _dot(o_ref_in[...].astype(BF16), wout_ref[...])


def _proj_residual(layer, x, o, wout):
    b = x.shape[0]
    return pl.pallas_call(
        _proj_residual_kernel,
        grid=(1,),
        in_specs=[_const_spec((b, D_MODEL)), _const_spec((b, Q_DIM)), _layer_spec((Q_DIM, D_MODEL), layer)],
        out_specs=pl.BlockSpec((b, D_MODEL), lambda i: (0, 0)),
        out_shape=jax.ShapeDtypeStruct((b, D_MODEL), F32),
        compiler_params=_params(1),
        name="proj_residual",
    )(x, o, wout)


def kernel(x_prompt, x_sample, state_rglru_h, state_rglru_conv, cache_swa_k, cache_swa_v, norm_mix_g, norm_mlp_g, lru_w_in, lru_conv_w, lru_conv_b, lru_w_a, lru_b_a, lru_w_x, lru_b_x, lru_lambda, lru_w_out, attn_w_qkv, attn_q_norm, attn_k_norm, attn_sinks, attn_w_out, mlp_w_up, mlp_w_down):
    n_p, t_p, _ = x_prompt.shape
    n_s = x_sample.shape[0]
    w_buf = cache_swa_k.shape[2]
    heads = jnp.arange(1, N_HEADS + 1, dtype=F32)
    slopes = jnp.exp2(-8.0 * heads / N_HEADS)
    slopes_b = jnp.broadcast_to(slopes[:, None], (N_HEADS, 128))
    head_of_lane = jnp.arange(KV_DIM) // HEAD_DIM
    head_ones = (head_of_lane[:, None] == head_of_lane[None, :]).astype(BF16)

    to_feature_major = lambda c: c.transpose(0, 1, 3, 4, 2).reshape(c.shape[0], n_s, KV_DIM, w_buf)
    from_feature_major = lambda c: c.reshape(c.shape[0], n_s, N_KV, HEAD_DIM, w_buf).transpose(0, 1, 4, 2, 3)
    kct, vct = to_feature_major(cache_swa_k), to_feature_major(cache_swa_v)

    win = lru_w_in.astype(BF16)
    wax = jnp.concatenate([lru_w_a, lru_w_x], axis=-1).astype(BF16)
    wlo = lru_w_out.astype(BF16)
    wqkv = attn_w_qkv.astype(BF16)
    wkvt = attn_w_qkv[:, :, Q_DIM:].transpose(0, 2, 1).astype(BF16)
    wao = attn_w_out.astype(BF16)
    wu = mlp_w_up.astype(BF16)
    wd = mlp_w_down.astype(BF16)

    row = lambda v: v.reshape(1, -1)
    yp = x_prompt
    ys = x_sample.reshape(n_s, D_MODEL)
    h_p, c_p, k_p, v_p, h_s, c_s, k_s, v_s = ([] for _ in range(8))
    for layer in range(DEPTH):
        j = layer // 2
        g_mix = row(norm_mix_g[layer])
        if layer % 2 == 0:
            args = (g_mix, win, lru_conv_w[j], row(lru_conv_b[j]), wax, row(lru_b_a[j]), row(lru_b_x[j]),
                    row(lru_lambda[j]), wlo)
            ys, hs, cs = _lru_sample(j, ys, state_rglru_h[j], state_rglru_conv[j].transpose(1, 0, 2), *args)
            yp, ys, hp, cp = _lru_mlp(j, layer, yp, ys, *args, row(norm_mlp_g[layer]), wu, wd)
            h_p.append(hp.reshape(n_p, D_RNN)); c_p.append(cp)
            h_s.append(hs); c_s.append(cs.transpose(1, 0, 2))
        else:
            qg, kg = row(attn_q_norm[j]), row(attn_k_norm[j])
            sinks = attn_sinks[j]
            kgt = jnp.broadcast_to(attn_k_norm[j][:, None], (HEAD_DIM, n_s))
            q, kn, vn, knt, vnt = _swa_sample_qkv(j, ys, g_mix, wqkv, wkvt, qg, kg, kgt)
            o3, ks, vs = _swa_sample_attn(
                j, q.reshape(n_s, N_HEADS, HEAD_DIM), kn.reshape(n_s, 1, KV_DIM), vn.reshape(n_s, 1, KV_DIM),
                knt, vnt, kct, vct, jnp.broadcast_to(sinks[:, None], (N_HEADS, 128)), slopes_b)
            k_s.append(ks); v_s.append(vs)
            ys = _proj_residual(j, ys, o3.reshape(n_s, Q_DIM), wao)
            yp, ys, kp, vp = _swa_mlp(j, layer, yp, ys, sinks, slopes, g_mix, wqkv, jnp.tile(qg, (1, GROUP)),
                                      jnp.tile(kg, (1, N_KV)), head_ones, wao, row(norm_mlp_g[layer]), wu, wd)
            k_p.append(kp.reshape(n_p, w_buf, N_KV, HEAD_DIM)); v_p.append(vp.reshape(n_p, w_buf, N_KV, HEAD_DIM))
    return (yp, ys.reshape(n_s, 1, D_MODEL),
            jnp.stack(h_p), jnp.stack(c_p), jnp.stack(k_p), jnp.stack(v_p),
            jnp.stack(h_s), jnp.stack(c_s), from_feature_major(jnp.stack(k_s)), from_feature_major(jnp.stack(v_s)))
```

```python
import functools

import jax
import jax.numpy as jnp
from jax import lax
from jax.experimental import pallas as pl
from jax.experimental.pallas import tpu as pltpu

F32 = jnp.float32
BF16 = jnp.bfloat16

D_MODEL = 1024
D_RNN = 1024
D_FF = 4096
DEPTH = 4
N_A = (DEPTH + 1) // 2
N_B = DEPTH // 2
CONV_W = 4
LRU_BLOCKS = 4
LRU_BW = D_RNN // LRU_BLOCKS
LRU_C = 8.0
N_HEADS = 16
N_KV = 4
GROUP = N_HEADS // N_KV
HEAD_DIM = 64
Q_DIM = N_HEADS * HEAD_DIM
KV_DIM = N_KV * HEAD_DIM
QKV_DIM = Q_DIM + 2 * KV_DIM
WINDOW = 128
assert GROUP * HEAD_DIM == KV_DIM and N_KV % 2 == 0 and GROUP % 2 == 0
RMS_EPS = 1e-6
NEG_INF = -1e30

V7X_VMEM_LIMIT_BYTES = 56 * 1024 * 1024
SUBLANES = 8
BF16_SUBLANES = 16
CAST_BLOCKS = 64

MLP_FC = 1024
LRU_TM = 256
SWA_TM = 256
SWA_SAMPLE_BS = 16
SWA_SAMPLE_UNROLL = 8


def _params(n_axes):
    return pltpu.CompilerParams(
        dimension_semantics=("arbitrary",) * n_axes,
        vmem_limit_bytes=V7X_VMEM_LIMIT_BYTES,
    )


def _const_spec(shape):
    nd = len(shape)
    return pl.BlockSpec(shape, lambda *_: (0,) * nd, pipeline_mode=pl.Buffered(1))


def _rms(x, g):
    return x * lax.rsqrt(jnp.mean(x * x, axis=-1, keepdims=True) + RMS_EPS) * g


def _dot(a, b):
    return jnp.dot(a, b, preferred_element_type=F32)


def _dot_nt(a, b):
    return lax.dot_general(a, b, (((1,), (1,)), ((), ())), preferred_element_type=F32)


def _cast_specs(sources):
    block = lambda i: jnp.minimum(i, CAST_BLOCKS - 1)
    in_specs, out_specs, out_shapes = [], [], []
    for w, layer in sources:
        _, r, c = w.shape
        rb = r // CAST_BLOCKS
        assert rb * CAST_BLOCKS == r and rb % BF16_SUBLANES == 0
        in_specs.append(pl.BlockSpec((None, rb, c), lambda i, layer=layer: (layer, block(i), 0)))
        out_specs.append(pl.BlockSpec((rb, c), lambda i: (block(i), 0)))
        out_shapes.append(jax.ShapeDtypeStruct((r, c), BF16))
    return in_specs, out_specs, out_shapes


def _with_casts(kernel, n_in, n_out, n_cast, *refs):
    ins, src = refs[:n_in], refs[n_in:n_in + n_cast]
    outs = refs[n_in + n_cast:n_in + n_cast + n_out]
    dst = refs[n_in + n_cast + n_out:n_in + 2 * n_cast + n_out]
    kernel(*ins, *outs, *refs[n_in + 2 * n_cast + n_out:])
    for s_ref, d_ref in zip(src, dst):
        d_ref[...] = s_ref[...].astype(BF16)


def _softplus(x):
    return jnp.maximum(x, 0.0) + jnp.log1p(jnp.exp(-jnp.abs(x)))


def _lru_gates(xc_k, ga, ba_k, bx_k, sp_k):
    r = jax.nn.sigmoid(ga[:, :LRU_BW] + ba_k)
    i = jax.nn.sigmoid(ga[:, LRU_BW:] + bx_k)
    log_a = -LRU_C * r * sp_k
    a = jnp.exp(log_a)
    u = jnp.sqrt(-jnp.tanh(log_a) * (a * a + 1.0)) * (i * xc_k)
    return a, u


def _scan_rows(a, u, h0):
    rows, c = a.shape
    groups = rows // SUBLANES
    a = a.reshape(groups, SUBLANES, c)
    u = u.reshape(groups, SUBLANES, c)
    row = lax.broadcasted_iota(jnp.int32, a.shape, 1)
    d = 1
    while d < SUBLANES:
        keep = row >= d
        u = jnp.where(keep, u + a * pltpu.roll(u, d, 1), u)
        a = jnp.where(keep, a * pltpu.roll(a, d, 1), a)
        d *= 2
    hs = []
    h = h0
    for g in range(groups):
        hg = a[g] * h + u[g]
        hs.append(hg)
        h = hg[SUBLANES - 1:SUBLANES, :]
    return jnp.concatenate(hs, axis=0)


def _lru_mlp_kernel(n_seq, x_ref, xs_ref, g_ref, win_ref, cw_ref, cb_ref, wax_ref, ba_ref, bx_ref, lam_ref,
                    wout_ref, g2_ref, wu_ref, wd_ref, o_ref, os_ref, h_ref, c_ref, xbuf, hcar, ymid):
    tm = x_ref.shape[1]
    nj = (pl.num_programs(0) - 1) // n_seq - 1
    i = pl.program_id(0)
    j = lax.rem(i, nj + 1)

    def mlp_chunk(xb, c):
        h = _dot(xb, wu_ref[:, c * MLP_FC:(c + 1) * MLP_FC])
        return jnp.square(jnp.maximum(h, 0.0)).astype(BF16)

    def step(do_mixer, do_mlp):
        if do_mlp:
            yprev = ymid[...]
            yb = _rms(yprev, g2_ref[...]).astype(BF16)
            mlp = yprev
        if not do_mixer:
            for c in range(D_FF // MLP_FC):
                mlp = mlp + _dot(mlp_chunk(yb, c), wd_ref[c * MLP_FC:(c + 1) * MLP_FC, :])
            o_ref[0] = mlp
            return
        x = x_ref[0]
        xb = _rms(x, g_ref[...]).astype(BF16)
        u_in = _dot(xb, win_ref[...])
        xr = u_in[:, D_RNN:]
        xbuf[SUBLANES:SUBLANES + tm, :] = xr
        cw = cw_ref[...]
        xc = xbuf[pl.ds(SUBLANES - 3, tm), :] * cw[0:1, :]
        xc = xc + xbuf[pl.ds(SUBLANES - 2, tm), :] * cw[1:2, :]
        xc = xc + xbuf[pl.ds(SUBLANES - 1, tm), :] * cw[2:3, :]
        xc = xc + xr * cw[3:4, :]
        xc = xc + cb_ref[...]
        xbuf[0:SUBLANES, :] = xbuf[tm:tm + SUBLANES, :]

        sp = _softplus(-lam_ref[...])
        blocks = [slice(k * LRU_BW, (k + 1) * LRU_BW) for k in range(LRU_BLOCKS)]
        gates = [_dot(xc[:, cs].astype(BF16), wax_ref[k]) for k, cs in enumerate(blocks)]
        acc = x
        for k, cs in enumerate(blocks):
            if do_mlp:
                hk = mlp_chunk(yb, k)
            a, u = _lru_gates(xc[:, cs], gates[k], ba_ref[:, cs], bx_ref[:, cs], sp[:, cs])
            h = _scan_rows(a, u, hcar[:, cs])
            hcar[:, cs] = h[tm - 1:tm, :]
            y = h * jax.nn.gelu(u_in[:, cs])
            if do_mlp:
                mlp = mlp + _dot(hk, wd_ref[k * MLP_FC:(k + 1) * MLP_FC, :])
            acc = acc + _dot(y.astype(BF16), wout_ref[cs, :])
        ymid[...] = acc
        if do_mlp:
            o_ref[0] = mlp

    prompt = i < n_seq * (nj + 1)

    @pl.when(prompt & (j == 0))
    def _():
        xbuf[0:SUBLANES, :] = jnp.zeros((SUBLANES, D_RNN), F32)
        hcar[...] = jnp.zeros_like(hcar)
        step(True, False)

    pl.when(prompt & (j > 0) & (j < nj))(lambda: step(True, True))
    pl.when(prompt & (j == nj))(lambda: step(False, True))

    @pl.when(prompt & (j == nj - 1))
    def _():
        h_ref[0] = hcar[...]
        c_ref[0] = xbuf[pl.ds(SUBLANES - (CONV_W - 1), CONV_W - 1), :]

    @pl.when(i == n_seq * (nj + 1))
    def _():
        x = xs_ref[...]
        xb = _rms(x, g2_ref[...]).astype(BF16)
        acc = x
        for c in range(D_FF // MLP_FC):
            acc = acc + _dot(mlp_chunk(xb, c), wd_ref[c * MLP_FC:(c + 1) * MLP_FC, :])
        os_ref[...] = acc


def _lru_mlp(x, xs, g, win, cw, cb, wax, ba, bx, lam, wout, g2, wu, wd, casts=()):
    n, t, _ = x.shape
    rows_s = xs.shape[0]
    tm = LRU_TM
    assert D_FF // MLP_FC == LRU_BLOCKS
    nj = t // tm
    steps = nj + 1
    seq = lambda i: jnp.minimum(i // steps, n - 1)
    chunk_in = lambda i: jnp.where(i >= n * steps, nj - 1, jnp.minimum(lax.rem(i, steps), nj - 1))
    chunk_out = lambda i: jnp.where(i >= n * steps, nj - 1, jnp.maximum(lax.rem(i, steps) - 1, 0))
    sample_spec = pl.BlockSpec((rows_s, D_MODEL), lambda i: (0, 0))
    assert n * steps + 1 >= CAST_BLOCKS
    cast_in, cast_out, cast_shapes = _cast_specs(casts)
    return pl.pallas_call(
        functools.partial(_with_casts, functools.partial(_lru_mlp_kernel, n), 14, 4, len(casts)),
        grid=(n * steps + 1,),
        in_specs=[
            pl.BlockSpec((1, tm, D_MODEL), lambda i: (seq(i), chunk_in(i), 0)),
            sample_spec,
            _const_spec((1, D_MODEL)),
            _const_spec((D_MODEL, 2 * D_RNN)),
            _const_spec((CONV_W, D_RNN)),
            _const_spec((1, D_RNN)),
            _const_spec((LRU_BLOCKS, LRU_BW, 2 * LRU_BW)),
            _const_spec((1, D_RNN)),
            _const_spec((1, D_RNN)),
            _const_spec((1, D_RNN)),
            _const_spec((D_RNN, D_MODEL)),
            _const_spec((1, D_MODEL)),
            _const_spec((D_MODEL, D_FF)),
            _const_spec((D_FF, D_MODEL)),
        ] + cast_in,
        out_specs=[
            pl.BlockSpec((1, tm, D_MODEL), lambda i: (seq(i), chunk_out(i), 0)),
            sample_spec,
            pl.BlockSpec((1, 1, D_RNN), lambda i: (seq(i), 0, 0)),
            pl.BlockSpec((1, CONV_W - 1, D_RNN), lambda i: (seq(i), 0, 0)),
        ] + cast_out,
        out_shape=[
            jax.ShapeDtypeStruct((n, t, D_MODEL), F32),
            jax.ShapeDtypeStruct((rows_s, D_MODEL), F32),
            jax.ShapeDtypeStruct((n, 1, D_RNN), F32),
            jax.ShapeDtypeStruct((n, CONV_W - 1, D_RNN), F32),
        ] + cast_shapes,
        scratch_shapes=[
            pltpu.VMEM((SUBLANES + tm, D_RNN), F32),
            pltpu.VMEM((1, D_RNN), F32),
            pltpu.VMEM((tm, D_MODEL), F32),
        ],
        compiler_params=_params(1),
        name="lru_mlp",
    )(x, xs, g, win, cw, cb, wax, ba, bx, lam, wout, g2, wu, wd, *[w for w, _ in casts])


def _lru_sample_kernel(x_ref, h0_ref, c0_ref, g_ref, win_ref, cw_ref, cb_ref, wax_ref, ba_ref, bx_ref, lam_ref,
                       wout_ref, o_ref, h_ref, c_ref):
    x = x_ref[...]
    xb = _rms(x, g_ref[...]).astype(BF16)
    u_in = _dot(xb, win_ref[...])
    xr = u_in[:, D_RNN:]
    cw = cw_ref[...]
    xc = c0_ref[0] * cw[0:1, :]
    xc = xc + c0_ref[1] * cw[1:2, :]
    xc = xc + c0_ref[2] * cw[2:3, :]
    xc = xc + xr * cw[3:4, :]
    xc = xc + cb_ref[...]
    c_ref[0] = c0_ref[1]
    c_ref[1] = c0_ref[2]
    c_ref[2] = xr

    sp = _softplus(-lam_ref[...])
    acc = x
    for k in range(LRU_BLOCKS):
        cs = slice(k * LRU_BW, (k + 1) * LRU_BW)
        ga = _dot(xc[:, cs].astype(BF16), wax_ref[k])
        a, u = _lru_gates(xc[:, cs], ga, ba_ref[:, cs], bx_ref[:, cs], sp[:, cs])
        h = a * h0_ref[:, cs] + u
        h_ref[:, cs] = h
        y = h * jax.nn.gelu(u_in[:, cs])
        acc = acc + _dot(y.astype(BF16), wout_ref[cs, :])
    o_ref[...] = acc


def _lru_sample(x, h0, c0, g, win, cw, cb, wax, ba, bx, lam, wout):
    b = x.shape[0]
    shapes = [(b, D_MODEL), (b, D_RNN), (CONV_W - 1, b, D_RNN), (1, D_MODEL), (D_MODEL, 2 * D_RNN),
              (CONV_W, D_RNN), (1, D_RNN), (LRU_BLOCKS, LRU_BW, 2 * LRU_BW), (1, D_RNN), (1, D_RNN), (1, D_RNN),
              (D_RNN, D_MODEL)]
    return pl.pallas_call(
        _lru_sample_kernel,
        grid=(1,),
        in_specs=[_const_spec(s) for s in shapes],
        out_specs=[
            pl.BlockSpec((b, D_MODEL), lambda i: (0, 0)),
            pl.BlockSpec((b, D_RNN), lambda i: (0, 0)),
            pl.BlockSpec((CONV_W - 1, b, D_RNN), lambda i: (0, 0, 0)),
        ],
        out_shape=[
            jax.ShapeDtypeStruct((b, D_MODEL), F32),
            jax.ShapeDtypeStruct((b, D_RNN), F32),
            jax.ShapeDtypeStruct((CONV_W - 1, b, D_RNN), F32),
        ],
        compiler_params=_params(1),
        name="lru_sample",
    )(x, h0, c0, g, win, cw, cb, wax, ba, bx, lam, wout)


def _head_norm(z, g):
    return z * lax.rsqrt(jnp.mean(z * z, axis=-1, keepdims=True) + RMS_EPS) * g


def _heads_norm(z, ones, g):
    z2 = z * z
    hi = z2.astype(BF16)
    lo = (z2 - hi.astype(F32)).astype(BF16)
    ssq = _dot(hi, ones) + _dot(lo, ones)
    return z * lax.rsqrt(ssq * (1.0 / HEAD_DIM) + RMS_EPS) * g


_MLP_ORDER = (("up", 0), ("up", 1), ("down", 0), ("up", 2), ("down", 1), ("up", 3), ("down", 2), ("down", 3))


def _swa_mlp_kernel(n_seq, sink_ref, slope_ref, x_ref, xs_ref, g_ref, wqkv_ref, qg_ref, kg_ref, ones_ref,
                    wout_ref, g2_ref, wu_ref, wd_ref, o_ref, os_ref, k_ref, v_ref, k2, v2, qn, obuf, bias, ymid):
    i = pl.program_id(0)
    nj = (pl.num_programs(0) - 1) // n_seq - 1

    def mlp_up(xb, c):
        h = _dot(xb, wu_ref[:, c * MLP_FC:(c + 1) * MLP_FC])
        return jnp.square(jnp.maximum(h, 0.0)).astype(BF16)

    def mlp_down(hk, c):
        return _dot(hk, wd_ref[c * MLP_FC:(c + 1) * MLP_FC, :])

    step = functools.partial(_swa_mlp_step, i == 0, sink_ref, slope_ref, x_ref, g_ref, wqkv_ref, qg_ref, kg_ref,
                             ones_ref, wout_ref, g2_ref, mlp_up, mlp_down, o_ref, k_ref, v_ref, k2, v2, qn, obuf,
                             bias, ymid)
    prompt = i < n_seq * (nj + 1)
    j = lax.rem(i, nj + 1)
    pl.when(prompt & (j == 0))(lambda: step(True, False))
    pl.when(prompt & (j > 0) & (j < nj))(lambda: step(True, True))
    pl.when(prompt & (j == nj))(lambda: step(False, True))

    @pl.when(i == n_seq * (nj + 1))
    def _():
        x = xs_ref[...]
        xb = _rms(x, g2_ref[...]).astype(BF16)
        acc = x
        for c in range(D_FF // MLP_FC):
            acc = acc + mlp_down(mlp_up(xb, c), c)
        os_ref[...] = acc


_SWA_MLP_REFS = (13, 4, 6)


def _swa_mlp_stack_kernel(n_seq, n_layers, *refs):
    n_in, n_out, n_scr = _SWA_MLP_REFS
    n_stacked = 2
    ins, parts = refs[:n_in], refs[n_in:n_in + n_stacked * n_layers]
    outs = refs[n_in + len(parts):n_in + len(parts) + n_out]
    stacked = refs[n_in + len(parts) + n_out:n_in + len(parts) + n_out + n_stacked]
    scratch, sem = refs[-n_scr - 1:-1], refs[-1]
    copies = [pltpu.make_async_copy(parts[t * n_layers + l], stacked[t].at[l], sem.at[t * n_layers + l])
              for t in range(n_stacked) for l in range(n_layers)]

    @pl.when(pl.program_id(0) == 0)
    def _():
        for c in copies:
            c.start()

    _swa_mlp_kernel(n_seq, *ins, *outs, *scratch)

    @pl.when(pl.program_id(0) == pl.num_programs(0) - 1)
    def _():
        for c in copies:
            c.wait()


def _swa_mlp_step(first, sink_ref, slope_ref, x_ref, g_ref, wqkv_ref, qg_ref, kg_ref, ones_ref, wout_ref,
                  g2_ref, mlp_up, mlp_down, o_ref, k_ref, v_ref, k2, v2, qn, obuf, bias, ymid, do_mixer, do_mlp):
    tm = x_ref.shape[1]
    nblk = tm // WINDOW
    first_chunk = not do_mlp

    if do_mlp:
        yprev = ymid[...]
        yb = _rms(yprev, g2_ref[...]).astype(BF16)
        mlp = yprev
    if not do_mixer:
        for c in range(D_FF // MLP_FC):
            mlp = mlp + mlp_down(mlp_up(yb, c), c)
        o_ref[0] = mlp
        return

    if first_chunk:
        @pl.when(first)
        def _():
            qi = lax.broadcasted_iota(jnp.int32, (WINDOW, 2 * WINDOW), 0)
            si = lax.broadcasted_iota(jnp.int32, (WINDOW, 2 * WINDOW), 1)
            dist = qi + WINDOW - si
            valid = (dist >= 0) & (dist < WINDOW)
            dist_f = dist.astype(F32)
            for h in range(N_HEADS):
                bias[h] = jnp.where(valid, -(slope_ref[h] * dist_f), NEG_INF)

        k2[:, :, 0:WINDOW, :] = jnp.zeros((N_KV, 2, WINDOW, 2 * HEAD_DIM), BF16)
        v2[:, :, 0:WINDOW, :] = jnp.zeros((N_KV, 2, WINDOW, 2 * HEAD_DIM), BF16)

    x = x_ref[0]
    xb = _rms(x, g_ref[...]).astype(BF16)
    qkv = _dot(xb, wqkv_ref[...])

    ones = ones_ref[...]
    kn = _heads_norm(qkv[:, Q_DIM:Q_DIM + KV_DIM], ones, kg_ref[...])
    vv = qkv[:, Q_DIM + KV_DIM:]
    k_ref[0] = kn[tm - WINDOW:, :]
    v_ref[0] = vv[tm - WINDOW:, :]
    low = lax.broadcasted_iota(jnp.int32, (tm, 2 * HEAD_DIM), 1) < HEAD_DIM
    for src, dst in ((kn, k2), (vv, v2)):
        for sl in range(N_KV // 2):
            z = src[:, sl * 2 * HEAD_DIM:(sl + 1) * 2 * HEAD_DIM]
            zs = pltpu.roll(z, HEAD_DIM, 1)
            dst[2 * sl, 0, WINDOW:WINDOW + tm, :] = jnp.where(low, z, 0.0).astype(BF16)
            dst[2 * sl, 1, WINDOW:WINDOW + tm, :] = jnp.where(low, 0.0, zs).astype(BF16)
            dst[2 * sl + 1, 0, WINDOW:WINDOW + tm, :] = jnp.where(low, zs, 0.0).astype(BF16)
            dst[2 * sl + 1, 1, WINDOW:WINDOW + tm, :] = jnp.where(low, 0.0, z).astype(BF16)
    qg = qg_ref[...] * (HEAD_DIM ** -0.5)
    for kv in range(N_KV):
        cs = slice(kv * KV_DIM, (kv + 1) * KV_DIM)
        qn[:, cs] = _heads_norm(qkv[:, cs], ones, qg).astype(BF16)

    col = lax.broadcasted_iota(jnp.int32, (1, 2 * WINDOW), 1)
    no_prev = jnp.where(col < WINDOW, NEG_INF, 0.0)

    def scores(kv, i):
        rq = slice(i * WINDOW, (i + 1) * WINDOW)
        ql = jnp.concatenate([qn[rq, kv * KV_DIM + p * 2 * HEAD_DIM:kv * KV_DIM + (p + 1) * 2 * HEAD_DIM]
                              for p in range(GROUP // 2)], axis=0)
        return [_dot_nt(ql, k2[kv, c, i * WINDOW:(i + 2) * WINDOW, :]) for c in range(2)]

    work = [(kv, i) for kv in range(N_KV) for i in range(nblk)]
    hidden = {}
    s_next = scores(*work[0])
    for n, (kv, i) in enumerate(work):
        mlp_ops = _MLP_ORDER[n * len(_MLP_ORDER) // len(work):(n + 1) * len(_MLP_ORDER) // len(work)]
        for kind, c in mlp_ops if do_mlp else ():
            if kind == "up":
                hidden[c] = mlp_up(yb, c)
            else:
                mlp = mlp + mlp_down(hidden.pop(c), c)
        rq = slice(i * WINDOW, (i + 1) * WINDOW)
        rk = slice(i * WINDOW, (i + 2) * WINDOW)
        s = s_next
        if n + 1 < len(work):
            s_next = scores(*work[n + 1])
        probs = [[None, None], [None, None]]
        for g in range(GROUP):
            p, c = divmod(g, 2)
            h = kv * GROUP + g
            sink = sink_ref[h]
            sg = s[c][p * WINDOW:(p + 1) * WINDOW, :] + bias[h]
            if i == 0 and first_chunk:
                sg = sg + no_prev
            m = jnp.maximum(jnp.max(sg, axis=-1, keepdims=True), sink)
            e = jnp.exp(sg - m)
            pr = e / (jnp.sum(e, axis=-1, keepdims=True) + jnp.exp(sink - m))
            probs[c][p] = pr.astype(BF16)
        pv = (_dot(jnp.concatenate(probs[0], axis=0), v2[kv, 0, rk, :])
              + _dot(jnp.concatenate(probs[1], axis=0), v2[kv, 1, rk, :]))
        for p in range(GROUP // 2):
            obuf[rq, kv * KV_DIM + p * 2 * HEAD_DIM:kv * KV_DIM + (p + 1) * 2 * HEAD_DIM] = (
                pv[p * WINDOW:(p + 1) * WINDOW, :].astype(BF16))

    assert not hidden
    k2[:, :, 0:WINDOW, :] = k2[:, :, tm:tm + WINDOW, :]
    v2[:, :, 0:WINDOW, :] = v2[:, :, tm:tm + WINDOW, :]
    ymid[...] = x + _dot(obuf[...], wout_ref[...])
    if do_mlp:
        o_ref[0] = mlp


def _swa_mlp(x, xs, sinks, slopes, g, wqkv, qg4, kg4, ones, wout, g2, wu, wd, stack=(), casts=()):
    n, t, _ = x.shape
    rows_s = xs.shape[0]
    tm = SWA_TM
    nj = t // tm
    steps = nj + 1
    seq = lambda i: jnp.minimum(i // steps, n - 1)
    chunk_in = lambda i: jnp.where(i >= n * steps, nj - 1, jnp.minimum(lax.rem(i, steps), nj - 1))
    chunk_out = lambda i: jnp.where(i >= n * steps, nj - 1, jnp.maximum(lax.rem(i, steps) - 1, 0))
    sample_spec = pl.BlockSpec((rows_s, D_MODEL), lambda i: (0, 0))
    smem = pl.BlockSpec(memory_space=pltpu.SMEM)
    hbm = pl.BlockSpec(memory_space=pl.ANY)
    parts = [p for group in stack for p in group]
    n_layers = len(stack[0]) if stack else 0
    body = functools.partial(_swa_mlp_stack_kernel, n, n_layers) if stack else functools.partial(_swa_mlp_kernel, n)
    n_in, n_out, _ = _SWA_MLP_REFS
    assert n * steps + 1 >= CAST_BLOCKS
    cast_in, cast_out, cast_shapes = _cast_specs(casts)
    return pl.pallas_call(
        functools.partial(_with_casts, body, n_in + len(parts), n_out + len(stack), len(casts)),
        grid=(n * steps + 1,),
        in_specs=[
            smem,
            smem,
            pl.BlockSpec((1, tm, D_MODEL), lambda i: (seq(i), chunk_in(i), 0)),
            sample_spec,
            _const_spec((1, D_MODEL)),
            _const_spec((D_MODEL, QKV_DIM)),
            _const_spec((1, KV_DIM)),
            _const_spec((1, KV_DIM)),
            _const_spec((KV_DIM, KV_DIM)),
            _const_spec((Q_DIM, D_MODEL)),
            _const_spec((1, D_MODEL)),
            _const_spec((D_MODEL, D_FF)),
            _const_spec((D_FF, D_MODEL)),
        ] + [hbm] * len(parts) + cast_in,
        out_specs=[
            pl.BlockSpec((1, tm, D_MODEL), lambda i: (seq(i), chunk_out(i), 0)),
            sample_spec,
            pl.BlockSpec((1, WINDOW, KV_DIM), lambda i: (seq(i), 0, 0)),
            pl.BlockSpec((1, WINDOW, KV_DIM), lambda i: (seq(i), 0, 0)),
        ] + [hbm] * len(stack) + cast_out,
        out_shape=[
            jax.ShapeDtypeStruct((n, t, D_MODEL), F32),
            jax.ShapeDtypeStruct((rows_s, D_MODEL), F32),
            jax.ShapeDtypeStruct((n, WINDOW, KV_DIM), F32),
            jax.ShapeDtypeStruct((n, WINDOW, KV_DIM), F32),
        ] + [jax.ShapeDtypeStruct((n_layers,) + group[0].shape, group[0].dtype) for group in stack] + cast_shapes,
        scratch_shapes=[
            pltpu.VMEM((N_KV, 2, WINDOW + tm, 2 * HEAD_DIM), BF16),
            pltpu.VMEM((N_KV, 2, WINDOW + tm, 2 * HEAD_DIM), BF16),
            pltpu.VMEM((tm, Q_DIM), BF16),
            pltpu.VMEM((tm, Q_DIM), BF16),
            pltpu.VMEM((N_HEADS, WINDOW, 2 * WINDOW), F32),
            pltpu.VMEM((tm, D_MODEL), F32),
        ] + ([pltpu.SemaphoreType.DMA((len(parts),))] if stack else []),
        compiler_params=_params(1),
        name="swa_mlp",
    )(sinks, slopes, x, xs, g, wqkv, qg4, kg4, ones, wout, g2, wu, wd, *parts, *[w for w, _ in casts])


def _swa_sample_qkv_kernel(x_ref, g_ref, wqkv_ref, wkvt_ref, qg_ref, kg_ref, kgt_ref,
                           q_ref, k_ref, v_ref, kt_ref, vt_ref):
    xb = _rms(x_ref[...], g_ref[...]).astype(BF16)
    qkv = _dot(xb, wqkv_ref[...])
    for h in range(N_HEADS):
        cs = slice(h * HEAD_DIM, (h + 1) * HEAD_DIM)
        q_ref[:, cs] = _head_norm(qkv[:, cs], qg_ref[...])
    for kv in range(N_KV):
        cs = slice(kv * HEAD_DIM, (kv + 1) * HEAD_DIM)
        k_ref[:, cs] = _head_norm(qkv[:, Q_DIM + kv * HEAD_DIM:Q_DIM + (kv + 1) * HEAD_DIM], kg_ref[...])
    v_ref[...] = qkv[:, Q_DIM + KV_DIM:]
    kvt = _dot_nt(wkvt_ref[...], xb)
    for kv in range(N_KV):
        rs = slice(kv * HEAD_DIM, (kv + 1) * HEAD_DIM)
        z = kvt[rs, :]
        kt_ref[rs, :] = z * lax.rsqrt(jnp.mean(z * z, axis=0, keepdims=True) + RMS_EPS) * kgt_ref[...]
    vt_ref[...] = kvt[KV_DIM:, :]


def _swa_sample_qkv(x, g, wqkv, wkvt, qg, kg, kgt):
    b = x.shape[0]
    shapes = [(b, D_MODEL), (1, D_MODEL), (D_MODEL, QKV_DIM), (2 * KV_DIM, D_MODEL), (1, HEAD_DIM), (1, HEAD_DIM),
              (HEAD_DIM, b)]
    out_shapes = [(b, Q_DIM), (b, KV_DIM), (b, KV_DIM), (KV_DIM, b), (KV_DIM, b)]
    return pl.pallas_call(
        _swa_sample_qkv_kernel,
        grid=(1,),
        in_specs=[_const_spec(s) for s in shapes],
        out_specs=[pl.BlockSpec(s, lambda i: (0, 0)) for s in out_shapes],
        out_shape=[jax.ShapeDtypeStruct(s, F32) for s in out_shapes],
        compiler_params=_params(1),
        name="swa_sample_qkv",
    )(x, g, wqkv, wkvt, qg, kg, kgt)


def _swa_sample_attn_kernel(q_ref, kn_ref, vn_ref, knt_ref, vnt_ref, kc_ref, vc_ref, sink_ref, slope_ref,
                            o_ref, ko_ref, vo_ref):
    bs = q_ref.shape[0]
    w = kc_ref.shape[3]
    i = pl.program_id(0)
    hrow = lax.broadcasted_iota(jnp.int32, (N_HEADS, KV_DIM), 0)
    feat = lax.broadcasted_iota(jnp.int32, (N_HEADS, KV_DIM), 1)
    own = (hrow // GROUP) == (feat // HEAD_DIM)
    key = lax.broadcasted_iota(jnp.int32, (N_HEADS, w), 1)
    dist = w - key
    bias = jnp.where(dist < WINDOW, -(slope_ref[:, 0:1] * dist.astype(F32)), NEG_INF)
    sink = sink_ref[:, 0:1]
    lane = lax.broadcasted_iota(jnp.int32, (KV_DIM, w), 1)
    newest = lane == w - 1

    def scores(b):
        q = q_ref[b]
        qt = jnp.concatenate([q] * N_KV, axis=1)
        qm = (jnp.where(own, qt, 0.0) * (HEAD_DIM ** -0.5)).astype(BF16)
        return qm, _dot(qm, kc_ref[0, b].astype(BF16)) + bias

    def softmax(b, qm, s):
        kn = kn_ref[b].astype(BF16).astype(F32)
        s_new = jnp.sum(qm.astype(F32) * kn, axis=-1, keepdims=True)
        m = jnp.maximum(jnp.maximum(jnp.max(s, axis=-1, keepdims=True), s_new), sink)
        e = jnp.exp(s - m)
        e_new = jnp.exp(s_new - m)
        den = jnp.sum(e, axis=-1, keepdims=True) + e_new + jnp.exp(sink - m)
        return (e / den).astype(BF16), (e_new / den).astype(BF16).astype(F32)

    def values(b, p, p_new):
        vn = vn_ref[b].astype(BF16).astype(F32)
        return _dot_nt(p, vc_ref[0, b].astype(BF16)) + p_new * vn

    def emit(b, res):
        res = jnp.where(own, res, 0.0)
        o = res[:, 0:HEAD_DIM]
        for kv in range(1, N_KV):
            o = o + res[:, kv * HEAD_DIM:(kv + 1) * HEAD_DIM]
        o_ref[b] = o

    def shift_in(b):
        mine = lane == i * bs + b
        kcol = jnp.sum(jnp.where(mine, knt_ref[...], 0.0), axis=1, keepdims=True)
        vcol = jnp.sum(jnp.where(mine, vnt_ref[...], 0.0), axis=1, keepdims=True)
        ko_ref[b] = jnp.where(newest, kcol, pltpu.roll(kc_ref[0, b], w - 1, 1))
        vo_ref[b] = jnp.where(newest, vcol, pltpu.roll(vc_ref[0, b], w - 1, 1))

    def body(grp, carry):
        seqs = [grp * SWA_SAMPLE_UNROLL + u for u in range(SWA_SAMPLE_UNROLL)]
        staged = [scores(b) for b in seqs]
        probs = [softmax(b, qm, s) for b, (qm, s) in zip(seqs, staged)]
        for b in seqs:
            shift_in(b)
        outs = [values(b, p, p_new) for b, (p, p_new) in zip(seqs, probs)]
        for b, res in zip(seqs, outs):
            emit(b, res)
        return carry

    lax.fori_loop(0, bs // SWA_SAMPLE_UNROLL, body, 0)


def _swa_sample_attn(layer, q3, kn3, vn3, knt, vnt, kct, vct, sink_b, slope_b):
    _, b, _, w = kct.shape
    assert w == 128 and b == 128, "one lane tile of key positions; new-token columns indexed by lane"
    bs = SWA_SAMPLE_BS
    cache_spec = pl.BlockSpec((1, bs, KV_DIM, w), lambda i: (layer, i, 0, 0))
    return pl.pallas_call(
        _swa_sample_attn_kernel,
        grid=(b // bs,),
        in_specs=[
            pl.BlockSpec((bs, N_HEADS, HEAD_DIM), lambda i: (i, 0, 0)),
            pl.BlockSpec((bs, 1, KV_DIM), lambda i: (i, 0, 0)),
            pl.BlockSpec((bs, 1, KV_DIM), lambda i: (i, 0, 0)),
            _const_spec((KV_DIM, b)),
            _const_spec((KV_DIM, b)),
            cache_spec,
            cache_spec,
            _const_spec((N_HEADS, 128)),
            _const_spec((N_HEADS, 128)),
        ],
        out_specs=[
            pl.BlockSpec((bs, N_HEADS, HEAD_DIM), lambda i: (i, 0, 0)),
            pl.BlockSpec((bs, KV_DIM, w), lambda i: (i, 0, 0)),
            pl.BlockSpec((bs, KV_DIM, w), lambda i: (i, 0, 0)),
        ],
        out_shape=[
            jax.ShapeDtypeStruct((b, N_HEADS, HEAD_DIM), F32),
            jax.ShapeDtypeStruct((b, KV_DIM, w), F32),
            jax.ShapeDtypeStruct((b, KV_DIM, w), F32),
        ],
        compiler_params=_params(1),
        name="swa_sample_attn",
    )(q3, kn3, vn3, knt, vnt, kct, vct, sink_b, slope_b)


def _proj_residual_kernel(x_ref, o_ref_in, wout_ref, y_ref):
    y_ref[...] = x_ref[...] + _dot(o_ref_in[...].astype(BF16), wout_ref[...])


def _proj_residual(x, o, wout):
    b = x.shape[0]
    return pl.pallas_call(
        _proj_residual_kernel,
        grid=(1,),
        in_specs=[_const_spec((b, D_MODEL)), _const_spec((b, Q_DIM)), _const_spec((Q_DIM, D_MODEL))],
        out_specs=pl.BlockSpec((b, D_MODEL), lambda i: (0, 0)),
        out_shape=jax.ShapeDtypeStruct((b, D_MODEL), F32),
        compiler_params=_params(1),
        name="proj_residual",
    )(x, o, wout)


def kernel(x_prompt, x_sample, state_rglru_h, state_rglru_conv, cache_swa_k, cache_swa_v, norm_mix_g, norm_mlp_g, lru_w_in, lru_conv_w, lru_conv_b, lru_w_a, lru_b_a, lru_w_x, lru_b_x, lru_lambda, lru_w_out, attn_w_qkv, attn_q_norm, attn_k_norm, attn_sinks, attn_w_out, mlp_w_up, mlp_w_down):
    n_p, t_p, _ = x_prompt.shape
    n_s = x_sample.shape[0]
    w_buf = cache_swa_k.shape[2]
    heads = jnp.arange(1, N_HEADS + 1, dtype=F32)
    slopes = jnp.exp2(-8.0 * heads / N_HEADS)
    slopes_b = jnp.broadcast_to(slopes[:, None], (N_HEADS, 128))
    head_of_lane = jnp.arange(KV_DIM) // HEAD_DIM
    head_ones = (head_of_lane[:, None] == head_of_lane[None, :]).astype(BF16)

    to_feature_major = lambda c: c.transpose(0, 1, 3, 4, 2).reshape(c.shape[0], n_s, KV_DIM, w_buf)
    from_feature_major = lambda c: c.reshape(c.shape[0], n_s, N_KV, HEAD_DIM, w_buf).transpose(0, 1, 4, 2, 3)
    kct, vct = to_feature_major(cache_swa_k), to_feature_major(cache_swa_v)

    wax_f32 = jnp.concatenate([lru_w_a, lru_w_x], axis=-1).reshape(N_A, D_RNN, 2 * LRU_BW)
    wkvt = attn_w_qkv[:, :, Q_DIM:].transpose(0, 2, 1).astype(BF16)

    def weight_sources(layer):
        mixer = ([(lru_w_in, layer // 2), (wax_f32, layer // 2), (lru_w_out, layer // 2)] if layer % 2 == 0 else
                 [(attn_w_qkv, layer // 2), (attn_w_out, layer // 2)])
        return mixer + [(mlp_w_up, layer), (mlp_w_down, layer)]

    weights = [w[layer].astype(BF16) for w, layer in weight_sources(0)]

    row = lambda v: v.reshape(1, -1)
    yp = x_prompt
    ys = x_sample.reshape(n_s, D_MODEL)
    h_p, c_p, k_p, v_p, h_s, c_s, k_s, v_s = ([] for _ in range(8))
    for layer in range(DEPTH):
        j = layer // 2
        g_mix = row(norm_mix_g[layer])
        g_mlp = row(norm_mlp_g[layer])
        casts = weight_sources(layer + 1) if layer + 1 < DEPTH else []
        *mixer_w, wu, wd = weights
        if layer % 2 == 0:
            win, wax, wlo = mixer_w
            args = (g_mix, win, lru_conv_w[j], row(lru_conv_b[j]), wax.reshape(LRU_BLOCKS, LRU_BW, 2 * LRU_BW),
                    row(lru_b_a[j]), row(lru_b_x[j]), row(lru_lambda[j]), wlo)
            ys, hs, cs = _lru_sample(ys, state_rglru_h[j], state_rglru_conv[j].transpose(1, 0, 2), *args)
            yp, ys, hp, cp, *weights = _lru_mlp(yp, ys, *args, g_mlp, wu, wd, casts)
            h_p.append(hp.reshape(n_p, D_RNN)); c_p.append(cp)
            h_s.append(hs); c_s.append(cs.transpose(1, 0, 2))
        else:
            wqkv, wao = mixer_w
            qg, kg = row(attn_q_norm[j]), row(attn_k_norm[j])
            sinks = attn_sinks[j]
            kgt = jnp.broadcast_to(attn_k_norm[j][:, None], (HEAD_DIM, n_s))
            q, kn, vn, knt, vnt = _swa_sample_qkv(ys, g_mix, wqkv, wkvt[j], qg, kg, kgt)
            o3, ks, vs = _swa_sample_attn(
                j, q.reshape(n_s, N_HEADS, HEAD_DIM), kn.reshape(n_s, 1, KV_DIM), vn.reshape(n_s, 1, KV_DIM),
                knt, vnt, kct, vct, jnp.broadcast_to(sinks[:, None], (N_HEADS, 128)), slopes_b)
            k_s.append(ks); v_s.append(vs)
            ys = _proj_residual(ys, o3.reshape(n_s, Q_DIM), wao)
            stack = (k_s, v_s) if len(k_s) == N_B else ()
            yp, ys, kp, vp, *rest = _swa_mlp(
                yp, ys, sinks, slopes, g_mix, wqkv, jnp.tile(qg, (1, GROUP)), jnp.tile(kg, (1, N_KV)),
                head_ones, wao, g_mlp, wu, wd, stack, casts)
            stacked, weights = rest[:len(stack)], rest[len(stack):]
            k_p.append(kp.reshape(n_p, w_buf, N_KV, HEAD_DIM)); v_p.append(vp.reshape(n_p, w_buf, N_KV, HEAD_DIM))
    return (yp, ys.reshape(n_s, 1, D_MODEL),
            jnp.stack(h_p), jnp.stack(c_p), jnp.stack(k_p), jnp.stack(v_p),
            jnp.stack(h_s), jnp.stack(c_s), from_feature_major(stacked[0]), from_feature_major(stacked[1]))
```

```python
import functools

import jax
import jax.numpy as jnp
from jax import lax
from jax.experimental import pallas as pl
from jax.experimental.pallas import tpu as pltpu

F32 = jnp.float32
BF16 = jnp.bfloat16

D_MODEL = 1024
D_RNN = 1024
D_FF = 4096
DEPTH = 4
N_A = (DEPTH + 1) // 2
N_B = DEPTH // 2
CONV_W = 4
LRU_BLOCKS = 4
LRU_BW = D_RNN // LRU_BLOCKS
LRU_C = 8.0
N_HEADS = 16
N_KV = 4
GROUP = N_HEADS // N_KV
HEAD_DIM = 64
Q_DIM = N_HEADS * HEAD_DIM
KV_DIM = N_KV * HEAD_DIM
QKV_DIM = Q_DIM + 2 * KV_DIM
WINDOW = 128
assert GROUP * HEAD_DIM == KV_DIM and N_KV % 2 == 0 and GROUP % 2 == 0
RMS_EPS = 1e-6
NEG_INF = -1e30

V7X_VMEM_LIMIT_BYTES = 56 * 1024 * 1024
SUBLANES = 8
BF16_SUBLANES = 16
CAST_BLOCKS = 64
STACK_BLOCKS = 64

MLP_FC = 1024
LRU_TM = 256
SWA_TM = 256
SWA_SAMPLE_BS = 16
SWA_SAMPLE_UNROLL = 8


def _params(n_axes):
    return pltpu.CompilerParams(
        dimension_semantics=("arbitrary",) * n_axes,
        vmem_limit_bytes=V7X_VMEM_LIMIT_BYTES,
    )


def _const_spec(shape):
    nd = len(shape)
    return pl.BlockSpec(shape, lambda *_: (0,) * nd, pipeline_mode=pl.Buffered(1))


def _rms(x, g):
    return x * lax.rsqrt(jnp.mean(x * x, axis=-1, keepdims=True) + RMS_EPS) * g


def _dot(a, b):
    return jnp.dot(a, b, preferred_element_type=F32)


def _dot_nt(a, b):
    return lax.dot_general(a, b, (((1,), (1,)), ((), ())), preferred_element_type=F32)


def _cast_specs(sources):
    block = lambda i: jnp.minimum(i, CAST_BLOCKS - 1)
    in_specs, out_specs, out_shapes = [], [], []
    for w, layer in sources:
        _, r, c = w.shape
        rb = r // CAST_BLOCKS
        assert rb * CAST_BLOCKS == r and rb % BF16_SUBLANES == 0
        in_specs.append(pl.BlockSpec((None, rb, c), lambda i, layer=layer: (layer, block(i), 0)))
        out_specs.append(pl.BlockSpec((rb, c), lambda i: (block(i), 0)))
        out_shapes.append(jax.ShapeDtypeStruct((r, c), BF16))
    return in_specs, out_specs, out_shapes


def _stack_specs(groups):
    block = lambda i: jnp.minimum(i, STACK_BLOCKS - 1)
    in_specs, out_specs, out_shapes = [], [], []
    for group in groups:
        shape, n = group[0].shape, len(group)
        rb, rest = shape[0] // STACK_BLOCKS, shape[1:]
        assert rb * STACK_BLOCKS == shape[0]
        zeros = (0,) * len(rest)
        in_specs += [pl.BlockSpec((rb,) + rest, lambda i: (block(i),) + zeros)] * n
        out_specs.append(pl.BlockSpec((n, rb) + rest, lambda i: (0, block(i)) + zeros))
        out_shapes.append(jax.ShapeDtypeStruct((n,) + shape, group[0].dtype))
    return in_specs, out_specs, out_shapes


def _with_side_jobs(kernel, n_in, n_out, n_cast, group_sizes, *refs):
    n_parts = sum(group_sizes)
    ins, rest = refs[:n_in], refs[n_in:]
    src, parts, rest = rest[:n_cast], rest[n_cast:n_cast + n_parts], rest[n_cast + n_parts:]
    outs, dst, stacked = rest[:n_out], rest[n_out:n_out + n_cast], rest[n_out + n_cast:n_out + n_cast + len(group_sizes)]
    kernel(*ins, *outs, *rest[n_out + n_cast + len(group_sizes):])
    for s_ref, d_ref in zip(src, dst):
        d_ref[...] = s_ref[...].astype(BF16)
    for d_ref, size in zip(stacked, group_sizes):
        for l in range(size):
            d_ref[l] = parts[l][...]
        parts = parts[size:]


def _softplus(x):
    return jnp.maximum(x, 0.0) + jnp.log1p(jnp.exp(-jnp.abs(x)))


def _lru_gates(xc_k, ga, ba_k, bx_k, sp_k):
    r = jax.nn.sigmoid(ga[:, :LRU_BW] + ba_k)
    i = jax.nn.sigmoid(ga[:, LRU_BW:] + bx_k)
    log_a = -LRU_C * r * sp_k
    a = jnp.exp(log_a)
    u = jnp.sqrt(-jnp.tanh(log_a) * (a * a + 1.0)) * (i * xc_k)
    return a, u


def _scan_rows(a, u, h0):
    rows, c = a.shape
    groups = rows // SUBLANES
    a = a.reshape(groups, SUBLANES, c)
    u = u.reshape(groups, SUBLANES, c)
    row = lax.broadcasted_iota(jnp.int32, a.shape, 1)
    d = 1
    while d < SUBLANES:
        keep = row >= d
        u = jnp.where(keep, u + a * pltpu.roll(u, d, 1), u)
        a = jnp.where(keep, a * pltpu.roll(a, d, 1), a)
        d *= 2
    hs = []
    h = h0
    for g in range(groups):
        hg = a[g] * h + u[g]
        hs.append(hg)
        h = hg[SUBLANES - 1:SUBLANES, :]
    return jnp.concatenate(hs, axis=0)


def _lru_mlp_kernel(n_seq, x_ref, xs_ref, g_ref, win_ref, cw_ref, cb_ref, wax_ref, ba_ref, bx_ref, lam_ref,
                    wout_ref, g2_ref, wu_ref, wd_ref, o_ref, os_ref, h_ref, c_ref, xbuf, hcar, ymid):
    tm = x_ref.shape[1]
    nj = (pl.num_programs(0) - 1) // n_seq - 1
    i = pl.program_id(0)
    j = lax.rem(i, nj + 1)

    def mlp_chunk(xb, c):
        h = _dot(xb, wu_ref[:, c * MLP_FC:(c + 1) * MLP_FC])
        return jnp.square(jnp.maximum(h, 0.0)).astype(BF16)

    def step(do_mixer, do_mlp):
        if do_mlp:
            yprev = ymid[...]
            yb = _rms(yprev, g2_ref[...]).astype(BF16)
            mlp = yprev
        if not do_mixer:
            for c in range(D_FF // MLP_FC):
                mlp = mlp + _dot(mlp_chunk(yb, c), wd_ref[c * MLP_FC:(c + 1) * MLP_FC, :])
            o_ref[0] = mlp
            return
        x = x_ref[0]
        xb = _rms(x, g_ref[...]).astype(BF16)
        u_in = _dot(xb, win_ref[...])
        xr = u_in[:, D_RNN:]
        xbuf[SUBLANES:SUBLANES + tm, :] = xr
        cw = cw_ref[...]
        xc = xbuf[pl.ds(SUBLANES - 3, tm), :] * cw[0:1, :]
        xc = xc + xbuf[pl.ds(SUBLANES - 2, tm), :] * cw[1:2, :]
        xc = xc + xbuf[pl.ds(SUBLANES - 1, tm), :] * cw[2:3, :]
        xc = xc + xr * cw[3:4, :]
        xc = xc + cb_ref[...]
        xbuf[0:SUBLANES, :] = xbuf[tm:tm + SUBLANES, :]

        sp = _softplus(-lam_ref[...])
        blocks = [slice(k * LRU_BW, (k + 1) * LRU_BW) for k in range(LRU_BLOCKS)]
        gates = [_dot(xc[:, cs].astype(BF16), wax_ref[k]) for k, cs in enumerate(blocks)]
        acc = x
        for k, cs in enumerate(blocks):
            if do_mlp:
                hk = mlp_chunk(yb, k)
            a, u = _lru_gates(xc[:, cs], gates[k], ba_ref[:, cs], bx_ref[:, cs], sp[:, cs])
            h = _scan_rows(a, u, hcar[:, cs])
            hcar[:, cs] = h[tm - 1:tm, :]
            y = h * jax.nn.gelu(u_in[:, cs])
            if do_mlp:
                mlp = mlp + _dot(hk, wd_ref[k * MLP_FC:(k + 1) * MLP_FC, :])
            acc = acc + _dot(y.astype(BF16), wout_ref[cs, :])
        ymid[...] = acc
        if do_mlp:
            o_ref[0] = mlp

    prompt = i < n_seq * (nj + 1)

    @pl.when(prompt & (j == 0))
    def _():
        xbuf[0:SUBLANES, :] = jnp.zeros((SUBLANES, D_RNN), F32)
        hcar[...] = jnp.zeros_like(hcar)
        step(True, False)

    pl.when(prompt & (j > 0) & (j < nj))(lambda: step(True, True))
    pl.when(prompt & (j == nj))(lambda: step(False, True))

    @pl.when(prompt & (j == nj - 1))
    def _():
        h_ref[0] = hcar[...]
        c_ref[0] = xbuf[pl.ds(SUBLANES - (CONV_W - 1), CONV_W - 1), :]

    @pl.when(i == n_seq * (nj + 1))
    def _():
        x = xs_ref[...]
        xb = _rms(x, g2_ref[...]).astype(BF16)
        acc = x
        for c in range(D_FF // MLP_FC):
            acc = acc + _dot(mlp_chunk(xb, c), wd_ref[c * MLP_FC:(c + 1) * MLP_FC, :])
        os_ref[...] = acc


def _lru_mlp(x, xs, g, win, cw, cb, wax, ba, bx, lam, wout, g2, wu, wd, casts=()):
    n, t, _ = x.shape
    rows_s = xs.shape[0]
    tm = LRU_TM
    assert D_FF // MLP_FC == LRU_BLOCKS
    nj = t // tm
    steps = nj + 1
    seq = lambda i: jnp.minimum(i // steps, n - 1)
    chunk_in = lambda i: jnp.where(i >= n * steps, nj - 1, jnp.minimum(lax.rem(i, steps), nj - 1))
    chunk_out = lambda i: jnp.where(i >= n * steps, nj - 1, jnp.maximum(lax.rem(i, steps) - 1, 0))
    sample_spec = pl.BlockSpec((rows_s, D_MODEL), lambda i: (0, 0))
    assert n * steps + 1 >= CAST_BLOCKS
    cast_in, cast_out, cast_shapes = _cast_specs(casts)
    return pl.pallas_call(
        functools.partial(_with_side_jobs, functools.partial(_lru_mlp_kernel, n), 14, 4, len(casts), ()),
        grid=(n * steps + 1,),
        in_specs=[
            pl.BlockSpec((1, tm, D_MODEL), lambda i: (seq(i), chunk_in(i), 0)),
            sample_spec,
            _const_spec((1, D_MODEL)),
            _const_spec((D_MODEL, 2 * D_RNN)),
            _const_spec((CONV_W, D_RNN)),
            _const_spec((1, D_RNN)),
            _const_spec((LRU_BLOCKS, LRU_BW, 2 * LRU_BW)),
            _const_spec((1, D_RNN)),
            _const_spec((1, D_RNN)),
            _const_spec((1, D_RNN)),
            _const_spec((D_RNN, D_MODEL)),
            _const_spec((1, D_MODEL)),
            _const_spec((D_MODEL, D_FF)),
            _const_spec((D_FF, D_MODEL)),
        ] + cast_in,
        out_specs=[
            pl.BlockSpec((1, tm, D_MODEL), lambda i: (seq(i), chunk_out(i), 0)),
            sample_spec,
            pl.BlockSpec((1, 1, D_RNN), lambda i: (seq(i), 0, 0)),
            pl.BlockSpec((1, CONV_W - 1, D_RNN), lambda i: (seq(i), 0, 0)),
        ] + cast_out,
        out_shape=[
            jax.ShapeDtypeStruct((n, t, D_MODEL), F32),
            jax.ShapeDtypeStruct((rows_s, D_MODEL), F32),
            jax.ShapeDtypeStruct((n, 1, D_RNN), F32),
            jax.ShapeDtypeStruct((n, CONV_W - 1, D_RNN), F32),
        ] + cast_shapes,
        scratch_shapes=[
            pltpu.VMEM((SUBLANES + tm, D_RNN), F32),
            pltpu.VMEM((1, D_RNN), F32),
            pltpu.VMEM((tm, D_MODEL), F32),
        ],
        compiler_params=_params(1),
        name="lru_mlp",
    )(x, xs, g, win, cw, cb, wax, ba, bx, lam, wout, g2, wu, wd, *[w for w, _ in casts])


def _lru_sample_kernel(x_ref, h0_ref, c0_ref, g_ref, win_ref, cw_ref, cb_ref, wax_ref, ba_ref, bx_ref, lam_ref,
                       wout_ref, o_ref, h_ref, c_ref):
    x = x_ref[...]
    xb = _rms(x, g_ref[...]).astype(BF16)
    u_in = _dot(xb, win_ref[...])
    xr = u_in[:, D_RNN:]
    cw = cw_ref[...]
    xc = c0_ref[0] * cw[0:1, :]
    xc = xc + c0_ref[1] * cw[1:2, :]
    xc = xc + c0_ref[2] * cw[2:3, :]
    xc = xc + xr * cw[3:4, :]
    xc = xc + cb_ref[...]
    c_ref[0] = c0_ref[1]
    c_ref[1] = c0_ref[2]
    c_ref[2] = xr

    sp = _softplus(-lam_ref[...])
    acc = x
    for k in range(LRU_BLOCKS):
        cs = slice(k * LRU_BW, (k + 1) * LRU_BW)
        ga = _dot(xc[:, cs].astype(BF16), wax_ref[k])
        a, u = _lru_gates(xc[:, cs], ga, ba_ref[:, cs], bx_ref[:, cs], sp[:, cs])
        h = a * h0_ref[:, cs] + u
        h_ref[:, cs] = h
        y = h * jax.nn.gelu(u_in[:, cs])
        acc = acc + _dot(y.astype(BF16), wout_ref[cs, :])
    o_ref[...] = acc


def _lru_sample(x, h0, c0, g, win, cw, cb, wax, ba, bx, lam, wout):
    b = x.shape[0]
    shapes = [(b, D_MODEL), (b, D_RNN), (CONV_W - 1, b, D_RNN), (1, D_MODEL), (D_MODEL, 2 * D_RNN),
              (CONV_W, D_RNN), (1, D_RNN), (LRU_BLOCKS, LRU_BW, 2 * LRU_BW), (1, D_RNN), (1, D_RNN), (1, D_RNN),
              (D_RNN, D_MODEL)]
    return pl.pallas_call(
        _lru_sample_kernel,
        grid=(1,),
        in_specs=[_const_spec(s) for s in shapes],
        out_specs=[
            pl.BlockSpec((b, D_MODEL), lambda i: (0, 0)),
            pl.BlockSpec((b, D_RNN), lambda i: (0, 0)),
            pl.BlockSpec((CONV_W - 1, b, D_RNN), lambda i: (0, 0, 0)),
        ],
        out_shape=[
            jax.ShapeDtypeStruct((b, D_MODEL), F32),
            jax.ShapeDtypeStruct((b, D_RNN), F32),
            jax.ShapeDtypeStruct((CONV_W - 1, b, D_RNN), F32),
        ],
        compiler_params=_params(1),
        name="lru_sample",
    )(x, h0, c0, g, win, cw, cb, wax, ba, bx, lam, wout)


def _head_norm(z, g):
    return z * lax.rsqrt(jnp.mean(z * z, axis=-1, keepdims=True) + RMS_EPS) * g


def _heads_norm(z, ones, g):
    z2 = z * z
    hi = z2.astype(BF16)
    lo = (z2 - hi.astype(F32)).astype(BF16)
    ssq = _dot(hi, ones) + _dot(lo, ones)
    return z * lax.rsqrt(ssq * (1.0 / HEAD_DIM) + RMS_EPS) * g


_MLP_ORDER = (("up", 0), ("up", 1), ("down", 0), ("up", 2), ("down", 1), ("up", 3), ("down", 2), ("down", 3))


def _swa_mlp_kernel(n_seq, sink_ref, slope_ref, x_ref, xs_ref, g_ref, wqkv_ref, qg_ref, kg_ref, ones_ref,
                    wout_ref, g2_ref, wu_ref, wd_ref, o_ref, os_ref, k_ref, v_ref, k2, v2, qn, obuf, bias, ymid):
    i = pl.program_id(0)
    nj = (pl.num_programs(0) - 1) // n_seq - 1

    def mlp_up(xb, c):
        h = _dot(xb, wu_ref[:, c * MLP_FC:(c + 1) * MLP_FC])
        return jnp.square(jnp.maximum(h, 0.0)).astype(BF16)

    def mlp_down(hk, c):
        return _dot(hk, wd_ref[c * MLP_FC:(c + 1) * MLP_FC, :])

    step = functools.partial(_swa_mlp_step, i == 0, sink_ref, slope_ref, x_ref, g_ref, wqkv_ref, qg_ref, kg_ref,
                             ones_ref, wout_ref, g2_ref, mlp_up, mlp_down, o_ref, k_ref, v_ref, k2, v2, qn, obuf,
                             bias, ymid)
    prompt = i < n_seq * (nj + 1)
    j = lax.rem(i, nj + 1)
    pl.when(prompt & (j == 0))(lambda: step(True, False))
    pl.when(prompt & (j > 0) & (j < nj))(lambda: step(True, True))
    pl.when(prompt & (j == nj))(lambda: step(False, True))

    @pl.when(i == n_seq * (nj + 1))
    def _():
        x = xs_ref[...]
        xb = _rms(x, g2_ref[...]).astype(BF16)
        acc = x
        for c in range(D_FF // MLP_FC):
            acc = acc + mlp_down(mlp_up(xb, c), c)
        os_ref[...] = acc


def _swa_mlp_step(first, sink_ref, slope_ref, x_ref, g_ref, wqkv_ref, qg_ref, kg_ref, ones_ref, wout_ref,
                  g2_ref, mlp_up, mlp_down, o_ref, k_ref, v_ref, k2, v2, qn, obuf, bias, ymid, do_mixer, do_mlp):
    tm = x_ref.shape[1]
    nblk = tm // WINDOW
    first_chunk = not do_mlp

    if do_mlp:
        yprev = ymid[...]
        yb = _rms(yprev, g2_ref[...]).astype(BF16)
        mlp = yprev
    if not do_mixer:
        for c in range(D_FF // MLP_FC):
            mlp = mlp + mlp_down(mlp_up(yb, c), c)
        o_ref[0] = mlp
        return

    if first_chunk:
        @pl.when(first)
        def _():
            qi = lax.broadcasted_iota(jnp.int32, (WINDOW, 2 * WINDOW), 0)
            si = lax.broadcasted_iota(jnp.int32, (WINDOW, 2 * WINDOW), 1)
            dist = qi + WINDOW - si
            valid = (dist >= 0) & (dist < WINDOW)
            dist_f = dist.astype(F32)
            for h in range(N_HEADS):
                bias[h] = jnp.where(valid, -(slope_ref[h] * dist_f), NEG_INF)

        k2[:, :, 0:WINDOW, :] = jnp.zeros((N_KV, 2, WINDOW, 2 * HEAD_DIM), BF16)
        v2[:, :, 0:WINDOW, :] = jnp.zeros((N_KV, 2, WINDOW, 2 * HEAD_DIM), BF16)

    x = x_ref[0]
    xb = _rms(x, g_ref[...]).astype(BF16)
    qkv = _dot(xb, wqkv_ref[...])

    ones = ones_ref[...]
    kn = _heads_norm(qkv[:, Q_DIM:Q_DIM + KV_DIM], ones, kg_ref[...])
    vv = qkv[:, Q_DIM + KV_DIM:]
    k_ref[0] = kn[tm - WINDOW:, :]
    v_ref[0] = vv[tm - WINDOW:, :]
    low = lax.broadcasted_iota(jnp.int32, (tm, 2 * HEAD_DIM), 1) < HEAD_DIM
    for src, dst in ((kn, k2), (vv, v2)):
        for sl in range(N_KV // 2):
            z = src[:, sl * 2 * HEAD_DIM:(sl + 1) * 2 * HEAD_DIM]
            zs = pltpu.roll(z, HEAD_DIM, 1)
            dst[2 * sl, 0, WINDOW:WINDOW + tm, :] = jnp.where(low, z, 0.0).astype(BF16)
            dst[2 * sl, 1, WINDOW:WINDOW + tm, :] = jnp.where(low, 0.0, zs).astype(BF16)
            dst[2 * sl + 1, 0, WINDOW:WINDOW + tm, :] = jnp.where(low, zs, 0.0).astype(BF16)
            dst[2 * sl + 1, 1, WINDOW:WINDOW + tm, :] = jnp.where(low, 0.0, z).astype(BF16)
    qg = qg_ref[...] * (HEAD_DIM ** -0.5)
    for kv in range(N_KV):
        cs = slice(kv * KV_DIM, (kv + 1) * KV_DIM)
        qn[:, cs] = _heads_norm(qkv[:, cs], ones, qg).astype(BF16)

    col = lax.broadcasted_iota(jnp.int32, (1, 2 * WINDOW), 1)
    no_prev = jnp.where(col < WINDOW, NEG_INF, 0.0)

    def scores(kv, i):
        rq = slice(i * WINDOW, (i + 1) * WINDOW)
        ql = jnp.concatenate([qn[rq, kv * KV_DIM + p * 2 * HEAD_DIM:kv * KV_DIM + (p + 1) * 2 * HEAD_DIM]
                              for p in range(GROUP // 2)], axis=0)
        return [_dot_nt(ql, k2[kv, c, i * WINDOW:(i + 2) * WINDOW, :]) for c in range(2)]

    work = [(kv, i) for kv in range(N_KV) for i in range(nblk)]
    hidden = {}
    s_next = scores(*work[0])
    for n, (kv, i) in enumerate(work):
        mlp_ops = _MLP_ORDER[n * len(_MLP_ORDER) // len(work):(n + 1) * len(_MLP_ORDER) // len(work)]
        for kind, c in mlp_ops if do_mlp else ():
            if kind == "up":
                hidden[c] = mlp_up(yb, c)
            else:
                mlp = mlp + mlp_down(hidden.pop(c), c)
        rq = slice(i * WINDOW, (i + 1) * WINDOW)
        rk = slice(i * WINDOW, (i + 2) * WINDOW)
        s = s_next
        if n + 1 < len(work):
            s_next = scores(*work[n + 1])
        probs = [[None, None], [None, None]]
        for g in range(GROUP):
            p, c = divmod(g, 2)
            h = kv * GROUP + g
            sink = sink_ref[h]
            sg = s[c][p * WINDOW:(p + 1) * WINDOW, :] + bias[h]
            if i == 0 and first_chunk:
                sg = sg + no_prev
            m = jnp.maximum(jnp.max(sg, axis=-1, keepdims=True), sink)
            e = jnp.exp(sg - m)
            pr = e / (jnp.sum(e, axis=-1, keepdims=True) + jnp.exp(sink - m))
            probs[c][p] = pr.astype(BF16)
        pv = (_dot(jnp.concatenate(probs[0], axis=0), v2[kv, 0, rk, :])
              + _dot(jnp.concatenate(probs[1], axis=0), v2[kv, 1, rk, :]))
        for p in range(GROUP // 2):
            obuf[rq, kv * KV_DIM + p * 2 * HEAD_DIM:kv * KV_DIM + (p + 1) * 2 * HEAD_DIM] = (
                pv[p * WINDOW:(p + 1) * WINDOW, :].astype(BF16))

    assert not hidden
    k2[:, :, 0:WINDOW, :] = k2[:, :, tm:tm + WINDOW, :]
    v2[:, :, 0:WINDOW, :] = v2[:, :, tm:tm + WINDOW, :]
    ymid[...] = x + _dot(obuf[...], wout_ref[...])
    if do_mlp:
        o_ref[0] = mlp


def _swa_mlp(x, xs, sinks, slopes, g, wqkv, qg4, kg4, ones, wout, g2, wu, wd, stack=(), casts=()):
    n, t, _ = x.shape
    rows_s = xs.shape[0]
    tm = SWA_TM
    nj = t // tm
    steps = nj + 1
    seq = lambda i: jnp.minimum(i // steps, n - 1)
    chunk_in = lambda i: jnp.where(i >= n * steps, nj - 1, jnp.minimum(lax.rem(i, steps), nj - 1))
    chunk_out = lambda i: jnp.where(i >= n * steps, nj - 1, jnp.maximum(lax.rem(i, steps) - 1, 0))
    sample_spec = pl.BlockSpec((rows_s, D_MODEL), lambda i: (0, 0))
    smem = pl.BlockSpec(memory_space=pltpu.SMEM)
    parts = [p for group in stack for p in group]
    assert n * steps + 1 >= max(CAST_BLOCKS, STACK_BLOCKS)
    cast_in, cast_out, cast_shapes = _cast_specs(casts)
    stack_in, stack_out, stack_shapes = _stack_specs(stack)
    return pl.pallas_call(
        functools.partial(_with_side_jobs, functools.partial(_swa_mlp_kernel, n), 13, 4, len(casts),
                          tuple(len(group) for group in stack)),
        grid=(n * steps + 1,),
        in_specs=[
            smem,
            smem,
            pl.BlockSpec((1, tm, D_MODEL), lambda i: (seq(i), chunk_in(i), 0)),
            sample_spec,
            _const_spec((1, D_MODEL)),
            _const_spec((D_MODEL, QKV_DIM)),
            _const_spec((1, KV_DIM)),
            _const_spec((1, KV_DIM)),
            _const_spec((KV_DIM, KV_DIM)),
            _const_spec((Q_DIM, D_MODEL)),
            _const_spec((1, D_MODEL)),
            _const_spec((D_MODEL, D_FF)),
            _const_spec((D_FF, D_MODEL)),
        ] + cast_in + stack_in,
        out_specs=[
            pl.BlockSpec((1, tm, D_MODEL), lambda i: (seq(i), chunk_out(i), 0)),
            sample_spec,
            pl.BlockSpec((1, WINDOW, KV_DIM), lambda i: (seq(i), 0, 0)),
            pl.BlockSpec((1, WINDOW, KV_DIM), lambda i: (seq(i), 0, 0)),
        ] + cast_out + stack_out,
        out_shape=[
            jax.ShapeDtypeStruct((n, t, D_MODEL), F32),
            jax.ShapeDtypeStruct((rows_s, D_MODEL), F32),
            jax.ShapeDtypeStruct((n, WINDOW, KV_DIM), F32),
            jax.ShapeDtypeStruct((n, WINDOW, KV_DIM), F32),
        ] + cast_shapes + stack_shapes,
        scratch_shapes=[
            pltpu.VMEM((N_KV, 2, WINDOW + tm, 2 * HEAD_DIM), BF16),
            pltpu.VMEM((N_KV, 2, WINDOW + tm, 2 * HEAD_DIM), BF16),
            pltpu.VMEM((tm, Q_DIM), BF16),
            pltpu.VMEM((tm, Q_DIM), BF16),
            pltpu.VMEM((N_HEADS, WINDOW, 2 * WINDOW), F32),
            pltpu.VMEM((tm, D_MODEL), F32),
        ],
        compiler_params=_params(1),
        name="swa_mlp",
    )(sinks, slopes, x, xs, g, wqkv, qg4, kg4, ones, wout, g2, wu, wd, *[w for w, _ in casts], *parts)


def _swa_sample_qkv_kernel(x_ref, g_ref, wqkv_ref, wkvt_ref, qg_ref, kg_ref, kgt_ref,
                           q_ref, k_ref, v_ref, kt_ref, vt_ref):
    xb = _rms(x_ref[...], g_ref[...]).astype(BF16)
    qkv = _dot(xb, wqkv_ref[...])
    for h in range(N_HEADS):
        cs = slice(h * HEAD_DIM, (h + 1) * HEAD_DIM)
        q_ref[:, cs] = _head_norm(qkv[:, cs], qg_ref[...])
    for kv in range(N_KV):
        cs = slice(kv * HEAD_DIM, (kv + 1) * HEAD_DIM)
        k_ref[:, cs] = _head_norm(qkv[:, Q_DIM + kv * HEAD_DIM:Q_DIM + (kv + 1) * HEAD_DIM], kg_ref[...])
    v_ref[...] = qkv[:, Q_DIM + KV_DIM:]
    kvt = _dot_nt(wkvt_ref[...], xb)
    for kv in range(N_KV):
        rs = slice(kv * HEAD_DIM, (kv + 1) * HEAD_DIM)
        z = kvt[rs, :]
        kt_ref[rs, :] = z * lax.rsqrt(jnp.mean(z * z, axis=0, keepdims=True) + RMS_EPS) * kgt_ref[...]
    vt_ref[...] = kvt[KV_DIM:, :]


def _swa_sample_qkv(x, g, wqkv, wkvt, qg, kg, kgt):
    b = x.shape[0]
    shapes = [(b, D_MODEL), (1, D_MODEL), (D_MODEL, QKV_DIM), (2 * KV_DIM, D_MODEL), (1, HEAD_DIM), (1, HEAD_DIM),
              (HEAD_DIM, b)]
    out_shapes = [(b, Q_DIM), (b, KV_DIM), (b, KV_DIM), (KV_DIM, b), (KV_DIM, b)]
    return pl.pallas_call(
        _swa_sample_qkv_kernel,
        grid=(1,),
        in_specs=[_const_spec(s) for s in shapes],
        out_specs=[pl.BlockSpec(s, lambda i: (0, 0)) for s in out_shapes],
        out_shape=[jax.ShapeDtypeStruct(s, F32) for s in out_shapes],
        compiler_params=_params(1),
        name="swa_sample_qkv",
    )(x, g, wqkv, wkvt, qg, kg, kgt)


def _swa_sample_attn_kernel(q_ref, kn_ref, vn_ref, knt_ref, vnt_ref, kc_ref, vc_ref, sink_ref, slope_ref,
                            o_ref, ko_ref, vo_ref):
    bs = q_ref.shape[0]
    w = kc_ref.shape[3]
    i = pl.program_id(0)
    hrow = lax.broadcasted_iota(jnp.int32, (N_HEADS, KV_DIM), 0)
    feat = lax.broadcasted_iota(jnp.int32, (N_HEADS, KV_DIM), 1)
    own = (hrow // GROUP) == (feat // HEAD_DIM)
    key = lax.broadcasted_iota(jnp.int32, (N_HEADS, w), 1)
    dist = w - key
    bias = jnp.where(dist < WINDOW, -(slope_ref[:, 0:1] * dist.astype(F32)), NEG_INF)
    sink = sink_ref[:, 0:1]
    lane = lax.broadcasted_iota(jnp.int32, (KV_DIM, w), 1)
    newest = lane == w - 1

    def scores(b):
        q = q_ref[b]
        qt = jnp.concatenate([q] * N_KV, axis=1)
        qm = (jnp.where(own, qt, 0.0) * (HEAD_DIM ** -0.5)).astype(BF16)
        return qm, _dot(qm, kc_ref[0, b].astype(BF16)) + bias

    def softmax(b, qm, s):
        kn = kn_ref[b].astype(BF16).astype(F32)
        s_new = jnp.sum(qm.astype(F32) * kn, axis=-1, keepdims=True)
        m = jnp.maximum(jnp.maximum(jnp.max(s, axis=-1, keepdims=True), s_new), sink)
        e = jnp.exp(s - m)
        e_new = jnp.exp(s_new - m)
        den = jnp.sum(e, axis=-1, keepdims=True) + e_new + jnp.exp(sink - m)
        return (e / den).astype(BF16), (e_new / den).astype(BF16).astype(F32)

    def values(b, p, p_new):
        vn = vn_ref[b].astype(BF16).astype(F32)
        return _dot_nt(p, vc_ref[0, b].astype(BF16)) + p_new * vn

    def emit(b, res):
        res = jnp.where(own, res, 0.0)
        o = res[:, 0:HEAD_DIM]
        for kv in range(1, N_KV):
            o = o + res[:, kv * HEAD_DIM:(kv + 1) * HEAD_DIM]
        o_ref[b] = o

    def shift_in(b):
        mine = lane == i * bs + b
        kcol = jnp.sum(jnp.where(mine, knt_ref[...], 0.0), axis=1, keepdims=True)
        vcol = jnp.sum(jnp.where(mine, vnt_ref[...], 0.0), axis=1, keepdims=True)
        ko_ref[b] = jnp.where(newest, kcol, pltpu.roll(kc_ref[0, b], w - 1, 1))
        vo_ref[b] = jnp.where(newest, vcol, pltpu.roll(vc_ref[0, b], w - 1, 1))

    def body(grp, carry):
        seqs = [grp * SWA_SAMPLE_UNROLL + u for u in range(SWA_SAMPLE_UNROLL)]
        staged = [scores(b) for b in seqs]
        probs = [softmax(b, qm, s) for b, (qm, s) in zip(seqs, staged)]
        for b in seqs:
            shift_in(b)
        outs = [values(b, p, p_new) for b, (p, p_new) in zip(seqs, probs)]
        for b, res in zip(seqs, outs):
            emit(b, res)
        return carry

    lax.fori_loop(0, bs // SWA_SAMPLE_UNROLL, body, 0)


def _swa_sample_attn(layer, q3, kn3, vn3, knt, vnt, kct, vct, sink_b, slope_b):
    _, b, _, w = kct.shape
    assert w == 128 and b == 128, "one lane tile of key positions; new-token columns indexed by lane"
    bs = SWA_SAMPLE_BS
    cache_spec = pl.BlockSpec((1, bs, KV_DIM, w), lambda i: (layer, i, 0, 0))
    return pl.pallas_call(
        _swa_sample_attn_kernel,
        grid=(b // bs,),
        in_specs=[
            pl.BlockSpec((bs, N_HEADS, HEAD_DIM), lambda i: (i, 0, 0)),
            pl.BlockSpec((bs, 1, KV_DIM), lambda i: (i, 0, 0)),
            pl.BlockSpec((bs, 1, KV_DIM), lambda i: (i, 0, 0)),
            _const_spec((KV_DIM, b)),
            _const_spec((KV_DIM, b)),
            cache_spec,
            cache_spec,
            _const_spec((N_HEADS, 128)),
            _const_spec((N_HEADS, 128)),
        ],
        out_specs=[
            pl.BlockSpec((bs, N_HEADS, HEAD_DIM), lambda i: (i, 0, 0)),
            pl.BlockSpec((bs, KV_DIM, w), lambda i: (i, 0, 0)),
            pl.BlockSpec((bs, KV_DIM, w), lambda i: (i, 0, 0)),
        ],
        out_shape=[
            jax.ShapeDtypeStruct((b, N_HEADS, HEAD_DIM), F32),
            jax.ShapeDtypeStruct((b, KV_DIM, w), F32),
            jax.ShapeDtypeStruct((b, KV_DIM, w), F32),
        ],
        compiler_params=_params(1),
        name="swa_sample_attn",
    )(q3, kn3, vn3, knt, vnt, kct, vct, sink_b, slope_b)


def _proj_residual_kernel(x_ref, o_ref_in, wout_ref, y_ref):
    y_ref[...] = x_ref[...] + _dot(o_ref_in[...].astype(BF16), wout_ref[...])


def _proj_residual(x, o, wout):
    b = x.shape[0]
    return pl.pallas_call(
        _proj_residual_kernel,
        grid=(1,),
        in_specs=[_const_spec((b, D_MODEL)), _const_spec((b, Q_DIM)), _const_spec((Q_DIM, D_MODEL))],
        out_specs=pl.BlockSpec((b, D_MODEL), lambda i: (0, 0)),
        out_shape=jax.ShapeDtypeStruct((b, D_MODEL), F32),
        compiler_params=_params(1),
        name="proj_residual",
    )(x, o, wout)


def kernel(x_prompt, x_sample, state_rglru_h, state_rglru_conv, cache_swa_k, cache_swa_v, norm_mix_g, norm_mlp_g, lru_w_in, lru_conv_w, lru_conv_b, lru_w_a, lru_b_a, lru_w_x, lru_b_x, lru_lambda, lru_w_out, attn_w_qkv, attn_q_norm, attn_k_norm, attn_sinks, attn_w_out, mlp_w_up, mlp_w_down):
    n_p, t_p, _ = x_prompt.shape
    n_s = x_sample.shape[0]
    w_buf = cache_swa_k.shape[2]
    heads = jnp.arange(1, N_HEADS + 1, dtype=F32)
    slopes = jnp.exp2(-8.0 * heads / N_HEADS)
    slopes_b = jnp.broadcast_to(slopes[:, None], (N_HEADS, 128))
    head_of_lane = jnp.arange(KV_DIM) // HEAD_DIM
    head_ones = (head_of_lane[:, None] == head_of_lane[None, :]).astype(BF16)

    to_feature_major = lambda c: c.transpose(0, 1, 3, 4, 2).reshape(c.shape[0], n_s, KV_DIM, w_buf)
    from_feature_major = lambda c: c.reshape(c.shape[0], n_s, N_KV, HEAD_DIM, w_buf).transpose(0, 1, 4, 2, 3)
    kct, vct = to_feature_major(cache_swa_k), to_feature_major(cache_swa_v)

    wax_f32 = jnp.concatenate([lru_w_a, lru_w_x], axis=-1).reshape(N_A, D_RNN, 2 * LRU_BW)
    wkvt = attn_w_qkv[:, :, Q_DIM:].transpose(0, 2, 1).astype(BF16)

    def weight_sources(layer):
        mixer = ([(lru_w_in, layer // 2), (wax_f32, layer // 2), (lru_w_out, layer // 2)] if layer % 2 == 0 else
                 [(attn_w_qkv, layer // 2), (attn_w_out, layer // 2)])
        return mixer + [(mlp_w_up, layer), (mlp_w_down, layer)]

    weights = [w[layer].astype(BF16) for w, layer in weight_sources(0)]

    row = lambda v: v.reshape(1, -1)
    yp = x_prompt
    ys = x_sample.reshape(n_s, D_MODEL)
    h_p, c_p, k_p, v_p, h_s, c_s, k_s, v_s = ([] for _ in range(8))
    for layer in range(DEPTH):
        j = layer // 2
        g_mix = row(norm_mix_g[layer])
        g_mlp = row(norm_mlp_g[layer])
        casts = weight_sources(layer + 1) if layer + 1 < DEPTH else []
        *mixer_w, wu, wd = weights
        if layer % 2 == 0:
            win, wax, wlo = mixer_w
            args = (g_mix, win, lru_conv_w[j], row(lru_conv_b[j]), wax.reshape(LRU_BLOCKS, LRU_BW, 2 * LRU_BW),
                    row(lru_b_a[j]), row(lru_b_x[j]), row(lru_lambda[j]), wlo)
            ys, hs, cs = _lru_sample(ys, state_rglru_h[j], state_rglru_conv[j].transpose(1, 0, 2), *args)
            yp, ys, hp, cp, *weights = _lru_mlp(yp, ys, *args, g_mlp, wu, wd, casts)
            h_p.append(hp.reshape(n_p, D_RNN)); c_p.append(cp)
            h_s.append(hs); c_s.append(cs.transpose(1, 0, 2))
        else:
            wqkv, wao = mixer_w
            qg, kg = row(attn_q_norm[j]), row(attn_k_norm[j])
            sinks = attn_sinks[j]
            kgt = jnp.broadcast_to(attn_k_norm[j][:, None], (HEAD_DIM, n_s))
            q, kn, vn, knt, vnt = _swa_sample_qkv(ys, g_mix, wqkv, wkvt[j], qg, kg, kgt)
            o3, ks, vs = _swa_sample_attn(
                j, q.reshape(n_s, N_HEADS, HEAD_DIM), kn.reshape(n_s, 1, KV_DIM), vn.reshape(n_s, 1, KV_DIM),
                knt, vnt, kct, vct, jnp.broadcast_to(sinks[:, None], (N_HEADS, 128)), slopes_b)
            k_s.append(ks); v_s.append(vs)
            ys = _proj_residual(ys, o3.reshape(n_s, Q_DIM), wao)
            stack = (k_s, v_s) if len(k_s) == N_B else ()
            yp, ys, kp, vp, *rest = _swa_mlp(
                yp, ys, sinks, slopes, g_mix, wqkv, jnp.tile(qg, (1, GROUP)), jnp.tile(kg, (1, N_KV)),
                head_ones, wao, g_mlp, wu, wd, stack, casts)
            weights, stacked = rest[:len(casts)], rest[len(casts):]
            k_p.append(kp.reshape(n_p, w_buf, N_KV, HEAD_DIM)); v_p.append(vp.reshape(n_p, w_buf, N_KV, HEAD_DIM))
    return (yp, ys.reshape(n_s, 1, D_MODEL),
            jnp.stack(h_p), jnp.stack(c_p), jnp.stack(k_p), jnp.stack(v_p),
            jnp.stack(h_s), jnp.stack(c_s), from_feature_major(stacked[0]), from_feature_major(stacked[1]))
```

```python
import functools

import jax
import jax.numpy as jnp
from jax import lax
from jax.experimental import pallas as pl
from jax.experimental.pallas import tpu as pltpu

F32 = jnp.float32
BF16 = jnp.bfloat16

D_MODEL = 1024
D_RNN = 1024
D_FF = 4096
DEPTH = 4
N_A = (DEPTH + 1) // 2
N_B = DEPTH // 2
CONV_W = 4
LRU_BLOCKS = 4
LRU_BW = D_RNN // LRU_BLOCKS
LRU_C = 8.0
N_HEADS = 16
N_KV = 4
GROUP = N_HEADS // N_KV
HEAD_DIM = 64
Q_DIM = N_HEADS * HEAD_DIM
KV_DIM = N_KV * HEAD_DIM
QKV_DIM = Q_DIM + 2 * KV_DIM
WINDOW = 128
assert GROUP * HEAD_DIM == KV_DIM and N_KV % 2 == 0 and GROUP % 2 == 0
RMS_EPS = 1e-6
NEG_INF = -1e30

V7X_VMEM_LIMIT_BYTES = 56 * 1024 * 1024
SUBLANES = 8
BF16_SUBLANES = 16
CAST_BLOCKS = 64
STACK_BLOCKS = 64

MLP_FC = 1024
MLP_LEAD = 3
LRU_TM = 256
SWA_TM = 256
SWA_SAMPLE_BS = 16
SWA_SAMPLE_UNROLL = 8


def _params(n_axes):
    return pltpu.CompilerParams(
        dimension_semantics=("arbitrary",) * n_axes,
        vmem_limit_bytes=V7X_VMEM_LIMIT_BYTES,
    )


def _const_spec(shape):
    nd = len(shape)
    return pl.BlockSpec(shape, lambda *_: (0,) * nd, pipeline_mode=pl.Buffered(1))


def _rms(x, g):
    return x * lax.rsqrt(jnp.mean(x * x, axis=-1, keepdims=True) + RMS_EPS) * g


def _dot(a, b):
    return jnp.dot(a, b, preferred_element_type=F32)


def _dot_nt(a, b):
    return lax.dot_general(a, b, (((1,), (1,)), ((), ())), preferred_element_type=F32)


def _cast_specs(sources):
    block = lambda i: jnp.minimum(i, CAST_BLOCKS - 1)
    in_specs, out_specs, out_shapes = [], [], []
    for w, layer in sources:
        _, r, c = w.shape
        rb = r // CAST_BLOCKS
        assert rb * CAST_BLOCKS == r and rb % BF16_SUBLANES == 0
        in_specs.append(pl.BlockSpec((None, rb, c), lambda i, layer=layer: (layer, block(i), 0)))
        out_specs.append(pl.BlockSpec((rb, c), lambda i: (block(i), 0)))
        out_shapes.append(jax.ShapeDtypeStruct((r, c), BF16))
    return in_specs, out_specs, out_shapes


def _stack_specs(groups):
    block = lambda i: jnp.minimum(i, STACK_BLOCKS - 1)
    in_specs, out_specs, out_shapes = [], [], []
    for group in groups:
        shape, n = group[0].shape, len(group)
        rb, rest = shape[0] // STACK_BLOCKS, shape[1:]
        assert rb * STACK_BLOCKS == shape[0]
        zeros = (0,) * len(rest)
        in_specs += [pl.BlockSpec((rb,) + rest, lambda i: (block(i),) + zeros)] * n
        out_specs.append(pl.BlockSpec((n, rb) + rest, lambda i: (0, block(i)) + zeros))
        out_shapes.append(jax.ShapeDtypeStruct((n,) + shape, group[0].dtype))
    return in_specs, out_specs, out_shapes


def _with_side_jobs(kernel, n_in, n_out, n_cast, group_sizes, *refs):
    n_parts = sum(group_sizes)
    ins, rest = refs[:n_in], refs[n_in:]
    src, parts, rest = rest[:n_cast], rest[n_cast:n_cast + n_parts], rest[n_cast + n_parts:]
    outs, dst, stacked = rest[:n_out], rest[n_out:n_out + n_cast], rest[n_out + n_cast:n_out + n_cast + len(group_sizes)]
    kernel(*ins, *outs, *rest[n_out + n_cast + len(group_sizes):])
    for s_ref, d_ref in zip(src, dst):
        d_ref[...] = s_ref[...].astype(BF16)
    for d_ref, size in zip(stacked, group_sizes):
        for l in range(size):
            d_ref[l] = parts[l][...]
        parts = parts[size:]


def _softplus(x):
    return jnp.maximum(x, 0.0) + jnp.log1p(jnp.exp(-jnp.abs(x)))


def _lru_gates(xc_k, ga, ba_k, bx_k, sp_k):
    r = jax.nn.sigmoid(ga[:, :LRU_BW] + ba_k)
    i = jax.nn.sigmoid(ga[:, LRU_BW:] + bx_k)
    log_a = -LRU_C * r * sp_k
    a = jnp.exp(log_a)
    u = jnp.sqrt(-jnp.tanh(log_a) * (a * a + 1.0)) * (i * xc_k)
    return a, u


def _scan_rows(a, u, h0):
    rows, c = a.shape
    groups = rows // SUBLANES
    a = a.reshape(groups, SUBLANES, c)
    u = u.reshape(groups, SUBLANES, c)
    row = lax.broadcasted_iota(jnp.int32, a.shape, 1)
    d = 1
    while d < SUBLANES:
        keep = row >= d
        u = jnp.where(keep, u + a * pltpu.roll(u, d, 1), u)
        a = jnp.where(keep, a * pltpu.roll(a, d, 1), a)
        d *= 2
    hs = []
    h = h0
    for g in range(groups):
        hg = a[g] * h + u[g]
        hs.append(hg)
        h = hg[SUBLANES - 1:SUBLANES, :]
    return jnp.concatenate(hs, axis=0)


def _lru_mlp_kernel(n_seq, x_ref, xs_ref, g_ref, win_ref, cw_ref, cb_ref, wax_ref, ba_ref, bx_ref, lam_ref,
                    wout_ref, g2_ref, wu_ref, wd_ref, o_ref, os_ref, h_ref, c_ref, xbuf, hcar, ymid):
    tm = x_ref.shape[1]
    nj = (pl.num_programs(0) - 1) // n_seq - 1
    i = pl.program_id(0)
    j = lax.rem(i, nj + 1)

    def mlp_chunk(xb, c):
        h = _dot(xb, wu_ref[:, c * MLP_FC:(c + 1) * MLP_FC])
        return jnp.square(jnp.maximum(h, 0.0)).astype(BF16)

    def step(do_mixer, do_mlp):
        if do_mlp:
            yprev = ymid[...]
            yb = _rms(yprev, g2_ref[...]).astype(BF16)
            mlp = yprev
        if not do_mixer:
            for c in range(D_FF // MLP_FC):
                mlp = mlp + _dot(mlp_chunk(yb, c), wd_ref[c * MLP_FC:(c + 1) * MLP_FC, :])
            o_ref[0] = mlp
            return
        x = x_ref[0]
        xb = _rms(x, g_ref[...]).astype(BF16)
        u_in = _dot(xb, win_ref[...])
        hidden = {c: mlp_chunk(yb, c) for c in range(MLP_LEAD)} if do_mlp else {}
        xr = u_in[:, D_RNN:]
        xbuf[SUBLANES:SUBLANES + tm, :] = xr
        cw = cw_ref[...]
        xc = xbuf[pl.ds(SUBLANES - 3, tm), :] * cw[0:1, :]
        xc = xc + xbuf[pl.ds(SUBLANES - 2, tm), :] * cw[1:2, :]
        xc = xc + xbuf[pl.ds(SUBLANES - 1, tm), :] * cw[2:3, :]
        xc = xc + xr * cw[3:4, :]
        xc = xc + cb_ref[...]
        xbuf[0:SUBLANES, :] = xbuf[tm:tm + SUBLANES, :]

        sp = _softplus(-lam_ref[...])
        blocks = [slice(k * LRU_BW, (k + 1) * LRU_BW) for k in range(LRU_BLOCKS)]
        gates = [_dot(xc[:, cs].astype(BF16), wax_ref[k]) for k, cs in enumerate(blocks)]
        acc = x
        for k, cs in enumerate(blocks):
            if do_mlp and k + MLP_LEAD < LRU_BLOCKS:
                hidden[k + MLP_LEAD] = mlp_chunk(yb, k + MLP_LEAD)
            a, u = _lru_gates(xc[:, cs], gates[k], ba_ref[:, cs], bx_ref[:, cs], sp[:, cs])
            h = _scan_rows(a, u, hcar[:, cs])
            hcar[:, cs] = h[tm - 1:tm, :]
            y = h * jax.nn.gelu(u_in[:, cs])
            if do_mlp:
                mlp = mlp + _dot(hidden.pop(k), wd_ref[k * MLP_FC:(k + 1) * MLP_FC, :])
            acc = acc + _dot(y.astype(BF16), wout_ref[cs, :])
        ymid[...] = acc
        if do_mlp:
            o_ref[0] = mlp

    prompt = i < n_seq * (nj + 1)

    @pl.when(prompt & (j == 0))
    def _():
        xbuf[0:SUBLANES, :] = jnp.zeros((SUBLANES, D_RNN), F32)
        hcar[...] = jnp.zeros_like(hcar)
        step(True, False)

    pl.when(prompt & (j > 0) & (j < nj))(lambda: step(True, True))
    pl.when(prompt & (j == nj))(lambda: step(False, True))

    @pl.when(prompt & (j == nj - 1))
    def _():
        h_ref[0] = hcar[...]
        c_ref[0] = xbuf[pl.ds(SUBLANES - (CONV_W - 1), CONV_W - 1), :]

    @pl.when(i == n_seq * (nj + 1))
    def _():
        x = xs_ref[...]
        xb = _rms(x, g2_ref[...]).astype(BF16)
        acc = x
        for c in range(D_FF // MLP_FC):
            acc = acc + _dot(mlp_chunk(xb, c), wd_ref[c * MLP_FC:(c + 1) * MLP_FC, :])
        os_ref[...] = acc


def _lru_mlp(x, xs, g, win, cw, cb, wax, ba, bx, lam, wout, g2, wu, wd, casts=()):
    n, t, _ = x.shape
    rows_s = xs.shape[0]
    tm = LRU_TM
    assert D_FF // MLP_FC == LRU_BLOCKS
    nj = t // tm
    steps = nj + 1
    seq = lambda i: jnp.minimum(i // steps, n - 1)
    chunk_in = lambda i: jnp.where(i >= n * steps, nj - 1, jnp.minimum(lax.rem(i, steps), nj - 1))
    chunk_out = lambda i: jnp.where(i >= n * steps, nj - 1, jnp.maximum(lax.rem(i, steps) - 1, 0))
    sample_spec = pl.BlockSpec((rows_s, D_MODEL), lambda i: (0, 0))
    assert n * steps + 1 >= CAST_BLOCKS
    cast_in, cast_out, cast_shapes = _cast_specs(casts)
    return pl.pallas_call(
        functools.partial(_with_side_jobs, functools.partial(_lru_mlp_kernel, n), 14, 4, len(casts), ()),
        grid=(n * steps + 1,),
        in_specs=[
            pl.BlockSpec((1, tm, D_MODEL), lambda i: (seq(i), chunk_in(i), 0)),
            sample_spec,
            _const_spec((1, D_MODEL)),
            _const_spec((D_MODEL, 2 * D_RNN)),
            _const_spec((CONV_W, D_RNN)),
            _const_spec((1, D_RNN)),
            _const_spec((LRU_BLOCKS, LRU_BW, 2 * LRU_BW)),
            _const_spec((1, D_RNN)),
            _const_spec((1, D_RNN)),
            _const_spec((1, D_RNN)),
            _const_spec((D_RNN, D_MODEL)),
            _const_spec((1, D_MODEL)),
            _const_spec((D_MODEL, D_FF)),
            _const_spec((D_FF, D_MODEL)),
        ] + cast_in,
        out_specs=[
            pl.BlockSpec((1, tm, D_MODEL), lambda i: (seq(i), chunk_out(i), 0)),
            sample_spec,
            pl.BlockSpec((1, 1, D_RNN), lambda i: (seq(i), 0, 0)),
            pl.BlockSpec((1, CONV_W - 1, D_RNN), lambda i: (seq(i), 0, 0)),
        ] + cast_out,
        out_shape=[
            jax.ShapeDtypeStruct((n, t, D_MODEL), F32),
            jax.ShapeDtypeStruct((rows_s, D_MODEL), F32),
            jax.ShapeDtypeStruct((n, 1, D_RNN), F32),
            jax.ShapeDtypeStruct((n, CONV_W - 1, D_RNN), F32),
        ] + cast_shapes,
        scratch_shapes=[
            pltpu.VMEM((SUBLANES + tm, D_RNN), F32),
            pltpu.VMEM((1, D_RNN), F32),
            pltpu.VMEM((tm, D_MODEL), F32),
        ],
        compiler_params=_params(1),
        name="lru_mlp",
    )(x, xs, g, win, cw, cb, wax, ba, bx, lam, wout, g2, wu, wd, *[w for w, _ in casts])


def _lru_sample_kernel(x_ref, h0_ref, c0_ref, g_ref, win_ref, cw_ref, cb_ref, wax_ref, ba_ref, bx_ref, lam_ref,
                       wout_ref, o_ref, h_ref, c_ref):
    x = x_ref[...]
    xb = _rms(x, g_ref[...]).astype(BF16)
    u_in = _dot(xb, win_ref[...])
    xr = u_in[:, D_RNN:]
    cw = cw_ref[...]
    xc = c0_ref[0] * cw[0:1, :]
    xc = xc + c0_ref[1] * cw[1:2, :]
    xc = xc + c0_ref[2] * cw[2:3, :]
    xc = xc + xr * cw[3:4, :]
    xc = xc + cb_ref[...]
    c_ref[0] = c0_ref[1]
    c_ref[1] = c0_ref[2]
    c_ref[2] = xr

    sp = _softplus(-lam_ref[...])
    acc = x
    for k in range(LRU_BLOCKS):
        cs = slice(k * LRU_BW, (k + 1) * LRU_BW)
        ga = _dot(xc[:, cs].astype(BF16), wax_ref[k])
        a, u = _lru_gates(xc[:, cs], ga, ba_ref[:, cs], bx_ref[:, cs], sp[:, cs])
        h = a * h0_ref[:, cs] + u
        h_ref[:, cs] = h
        y = h * jax.nn.gelu(u_in[:, cs])
        acc = acc + _dot(y.astype(BF16), wout_ref[cs, :])
    o_ref[...] = acc


def _lru_sample(x, h0, c0, g, win, cw, cb, wax, ba, bx, lam, wout):
    b = x.shape[0]
    shapes = [(b, D_MODEL), (b, D_RNN), (CONV_W - 1, b, D_RNN), (1, D_MODEL), (D_MODEL, 2 * D_RNN),
              (CONV_W, D_RNN), (1, D_RNN), (LRU_BLOCKS, LRU_BW, 2 * LRU_BW), (1, D_RNN), (1, D_RNN), (1, D_RNN),
              (D_RNN, D_MODEL)]
    return pl.pallas_call(
        _lru_sample_kernel,
        grid=(1,),
        in_specs=[_const_spec(s) for s in shapes],
        out_specs=[
            pl.BlockSpec((b, D_MODEL), lambda i: (0, 0)),
            pl.BlockSpec((b, D_RNN), lambda i: (0, 0)),
            pl.BlockSpec((CONV_W - 1, b, D_RNN), lambda i: (0, 0, 0)),
        ],
        out_shape=[
            jax.ShapeDtypeStruct((b, D_MODEL), F32),
            jax.ShapeDtypeStruct((b, D_RNN), F32),
            jax.ShapeDtypeStruct((CONV_W - 1, b, D_RNN), F32),
        ],
        compiler_params=_params(1),
        name="lru_sample",
    )(x, h0, c0, g, win, cw, cb, wax, ba, bx, lam, wout)


def _head_norm(z, g):
    return z * lax.rsqrt(jnp.mean(z * z, axis=-1, keepdims=True) + RMS_EPS) * g


def _heads_norm(z, ones, g):
    z2 = z * z
    hi = z2.astype(BF16)
    lo = (z2 - hi.astype(F32)).astype(BF16)
    ssq = _dot(hi, ones) + _dot(lo, ones)
    return z * lax.rsqrt(ssq * (1.0 / HEAD_DIM) + RMS_EPS) * g


_MLP_ORDER = (("up", 0), ("up", 1), ("down", 0), ("up", 2), ("down", 1), ("up", 3), ("down", 2), ("down", 3))


def _swa_mlp_kernel(n_seq, sink_ref, slope_ref, x_ref, xs_ref, g_ref, wqkv_ref, qg_ref, kg_ref, ones_ref,
                    wout_ref, g2_ref, wu_ref, wd_ref, o_ref, os_ref, k_ref, v_ref, k2, v2, qn, obuf, bias, ymid):
    i = pl.program_id(0)
    nj = (pl.num_programs(0) - 1) // n_seq - 1

    def mlp_up(xb, c):
        h = _dot(xb, wu_ref[:, c * MLP_FC:(c + 1) * MLP_FC])
        return jnp.square(jnp.maximum(h, 0.0)).astype(BF16)

    def mlp_down(hk, c):
        return _dot(hk, wd_ref[c * MLP_FC:(c + 1) * MLP_FC, :])

    step = functools.partial(_swa_mlp_step, i == 0, sink_ref, slope_ref, x_ref, g_ref, wqkv_ref, qg_ref, kg_ref,
                             ones_ref, wout_ref, g2_ref, mlp_up, mlp_down, o_ref, k_ref, v_ref, k2, v2, qn, obuf,
                             bias, ymid)
    prompt = i < n_seq * (nj + 1)
    j = lax.rem(i, nj + 1)
    pl.when(prompt & (j == 0))(lambda: step(True, False))
    pl.when(prompt & (j > 0) & (j < nj))(lambda: step(True, True))
    pl.when(prompt & (j == nj))(lambda: step(False, True))

    @pl.when(i == n_seq * (nj + 1))
    def _():
        x = xs_ref[...]
        xb = _rms(x, g2_ref[...]).astype(BF16)
        acc = x
        for c in range(D_FF // MLP_FC):
            acc = acc + mlp_down(mlp_up(xb, c), c)
        os_ref[...] = acc


def _swa_mlp_step(first, sink_ref, slope_ref, x_ref, g_ref, wqkv_ref, qg_ref, kg_ref, ones_ref, wout_ref,
                  g2_ref, mlp_up, mlp_down, o_ref, k_ref, v_ref, k2, v2, qn, obuf, bias, ymid, do_mixer, do_mlp):
    tm = x_ref.shape[1]
    nblk = tm // WINDOW
    first_chunk = not do_mlp

    if do_mlp:
        yprev = ymid[...]
        yb = _rms(yprev, g2_ref[...]).astype(BF16)
        mlp = yprev
    if not do_mixer:
        for c in range(D_FF // MLP_FC):
            mlp = mlp + mlp_down(mlp_up(yb, c), c)
        o_ref[0] = mlp
        return

    if first_chunk:
        @pl.when(first)
        def _():
            qi = lax.broadcasted_iota(jnp.int32, (WINDOW, 2 * WINDOW), 0)
            si = lax.broadcasted_iota(jnp.int32, (WINDOW, 2 * WINDOW), 1)
            dist = qi + WINDOW - si
            valid = (dist >= 0) & (dist < WINDOW)
            dist_f = dist.astype(F32)
            for h in range(N_HEADS):
                bias[h] = jnp.where(valid, -(slope_ref[h] * dist_f), NEG_INF)

        k2[:, :, 0:WINDOW, :] = jnp.zeros((N_KV, 2, WINDOW, 2 * HEAD_DIM), BF16)
        v2[:, :, 0:WINDOW, :] = jnp.zeros((N_KV, 2, WINDOW, 2 * HEAD_DIM), BF16)

    x = x_ref[0]
    xb = _rms(x, g_ref[...]).astype(BF16)
    qkv = _dot(xb, wqkv_ref[...])

    ones = ones_ref[...]
    kn = _heads_norm(qkv[:, Q_DIM:Q_DIM + KV_DIM], ones, kg_ref[...])
    vv = qkv[:, Q_DIM + KV_DIM:]
    k_ref[0] = kn[tm - WINDOW:, :]
    v_ref[0] = vv[tm - WINDOW:, :]
    low = lax.broadcasted_iota(jnp.int32, (tm, 2 * HEAD_DIM), 1) < HEAD_DIM
    for src, dst in ((kn, k2), (vv, v2)):
        for sl in range(N_KV // 2):
            z = src[:, sl * 2 * HEAD_DIM:(sl + 1) * 2 * HEAD_DIM]
            zs = pltpu.roll(z, HEAD_DIM, 1)
            dst[2 * sl, 0, WINDOW:WINDOW + tm, :] = jnp.where(low, z, 0.0).astype(BF16)
            dst[2 * sl, 1, WINDOW:WINDOW + tm, :] = jnp.where(low, 0.0, zs).astype(BF16)
            dst[2 * sl + 1, 0, WINDOW:WINDOW + tm, :] = jnp.where(low, zs, 0.0).astype(BF16)
            dst[2 * sl + 1, 1, WINDOW:WINDOW + tm, :] = jnp.where(low, 0.0, z).astype(BF16)
    qg = qg_ref[...] * (HEAD_DIM ** -0.5)
    for kv in range(N_KV):
        cs = slice(kv * KV_DIM, (kv + 1) * KV_DIM)
        qn[:, cs] = _heads_norm(qkv[:, cs], ones, qg).astype(BF16)

    col = lax.broadcasted_iota(jnp.int32, (1, 2 * WINDOW), 1)
    no_prev = jnp.where(col < WINDOW, NEG_INF, 0.0)

    def scores(kv, i):
        rq = slice(i * WINDOW, (i + 1) * WINDOW)
        ql = jnp.concatenate([qn[rq, kv * KV_DIM + p * 2 * HEAD_DIM:kv * KV_DIM + (p + 1) * 2 * HEAD_DIM]
                              for p in range(GROUP // 2)], axis=0)
        return [_dot_nt(ql, k2[kv, c, i * WINDOW:(i + 2) * WINDOW, :]) for c in range(2)]

    work = [(kv, i) for kv in range(N_KV) for i in range(nblk)]
    hidden = {}
    s_next = scores(*work[0])
    for n, (kv, i) in enumerate(work):
        mlp_ops = _MLP_ORDER[n * len(_MLP_ORDER) // len(work):(n + 1) * len(_MLP_ORDER) // len(work)]
        for kind, c in mlp_ops if do_mlp else ():
            if kind == "up":
                hidden[c] = mlp_up(yb, c)
            else:
                mlp = mlp + mlp_down(hidden.pop(c), c)
        rq = slice(i * WINDOW, (i + 1) * WINDOW)
        rk = slice(i * WINDOW, (i + 2) * WINDOW)
        s = s_next
        if n + 1 < len(work):
            s_next = scores(*work[n + 1])
        probs = [[None, None], [None, None]]
        for g in range(GROUP):
            p, c = divmod(g, 2)
            h = kv * GROUP + g
            sink = sink_ref[h]
            sg = s[c][p * WINDOW:(p + 1) * WINDOW, :] + bias[h]
            if i == 0 and first_chunk:
                sg = sg + no_prev
            m = jnp.maximum(jnp.max(sg, axis=-1, keepdims=True), sink)
            e = jnp.exp(sg - m)
            pr = e / (jnp.sum(e, axis=-1, keepdims=True) + jnp.exp(sink - m))
            probs[c][p] = pr.astype(BF16)
        pv = (_dot(jnp.concatenate(probs[0], axis=0), v2[kv, 0, rk, :])
              + _dot(jnp.concatenate(probs[1], axis=0), v2[kv, 1, rk, :]))
        for p in range(GROUP // 2):
            obuf[rq, kv * KV_DIM + p * 2 * HEAD_DIM:kv * KV_DIM + (p + 1) * 2 * HEAD_DIM] = (
                pv[p * WINDOW:(p + 1) * WINDOW, :].astype(BF16))

    assert not hidden
    k2[:, :, 0:WINDOW, :] = k2[:, :, tm:tm + WINDOW, :]
    v2[:, :, 0:WINDOW, :] = v2[:, :, tm:tm + WINDOW, :]
    ymid[...] = x + _dot(obuf[...], wout_ref[...])
    if do_mlp:
        o_ref[0] = mlp


def _swa_mlp(x, xs, sinks, slopes, g, wqkv, qg4, kg4, ones, wout, g2, wu, wd, stack=(), casts=()):
    n, t, _ = x.shape
    rows_s = xs.shape[0]
    tm = SWA_TM
    nj = t // tm
    steps = nj + 1
    seq = lambda i: jnp.minimum(i // steps, n - 1)
    chunk_in = lambda i: jnp.where(i >= n * steps, nj - 1, jnp.minimum(lax.rem(i, steps), nj - 1))
    chunk_out = lambda i: jnp.where(i >= n * steps, nj - 1, jnp.maximum(lax.rem(i, steps) - 1, 0))
    sample_spec = pl.BlockSpec((rows_s, D_MODEL), lambda i: (0, 0))
    smem = pl.BlockSpec(memory_space=pltpu.SMEM)
    parts = [p for group in stack for p in group]
    assert n * steps + 1 >= max(CAST_BLOCKS, STACK_BLOCKS)
    cast_in, cast_out, cast_shapes = _cast_specs(casts)
    stack_in, stack_out, stack_shapes = _stack_specs(stack)
    return pl.pallas_call(
        functools.partial(_with_side_jobs, functools.partial(_swa_mlp_kernel, n), 13, 4, len(casts),
                          tuple(len(group) for group in stack)),
        grid=(n * steps + 1,),
        in_specs=[
            smem,
            smem,
            pl.BlockSpec((1, tm, D_MODEL), lambda i: (seq(i), chunk_in(i), 0)),
            sample_spec,
            _const_spec((1, D_MODEL)),
            _const_spec((D_MODEL, QKV_DIM)),
            _const_spec((1, KV_DIM)),
            _const_spec((1, KV_DIM)),
            _const_spec((KV_DIM, KV_DIM)),
            _const_spec((Q_DIM, D_MODEL)),
            _const_spec((1, D_MODEL)),
            _const_spec((D_MODEL, D_FF)),
            _const_spec((D_FF, D_MODEL)),
        ] + cast_in + stack_in,
        out_specs=[
            pl.BlockSpec((1, tm, D_MODEL), lambda i: (seq(i), chunk_out(i), 0)),
            sample_spec,
            pl.BlockSpec((1, WINDOW, KV_DIM), lambda i: (seq(i), 0, 0)),
            pl.BlockSpec((1, WINDOW, KV_DIM), lambda i: (seq(i), 0, 0)),
        ] + cast_out + stack_out,
        out_shape=[
            jax.ShapeDtypeStruct((n, t, D_MODEL), F32),
            jax.ShapeDtypeStruct((rows_s, D_MODEL), F32),
            jax.ShapeDtypeStruct((n, WINDOW, KV_DIM), F32),
            jax.ShapeDtypeStruct((n, WINDOW, KV_DIM), F32),
        ] + cast_shapes + stack_shapes,
        scratch_shapes=[
            pltpu.VMEM((N_KV, 2, WINDOW + tm, 2 * HEAD_DIM), BF16),
            pltpu.VMEM((N_KV, 2, WINDOW + tm, 2 * HEAD_DIM), BF16),
            pltpu.VMEM((tm, Q_DIM), BF16),
            pltpu.VMEM((tm, Q_DIM), BF16),
            pltpu.VMEM((N_HEADS, WINDOW, 2 * WINDOW), F32),
            pltpu.VMEM((tm, D_MODEL), F32),
        ],
        compiler_params=_params(1),
        name="swa_mlp",
    )(sinks, slopes, x, xs, g, wqkv, qg4, kg4, ones, wout, g2, wu, wd, *[w for w, _ in casts], *parts)


def _swa_sample_qkv_kernel(x_ref, g_ref, wqkv_ref, wkvt_ref, qg_ref, kg_ref, kgt_ref,
                           q_ref, k_ref, v_ref, kt_ref, vt_ref):
    xb = _rms(x_ref[...], g_ref[...]).astype(BF16)
    qkv = _dot(xb, wqkv_ref[...])
    for h in range(N_HEADS):
        cs = slice(h * HEAD_DIM, (h + 1) * HEAD_DIM)
        q_ref[:, cs] = _head_norm(qkv[:, cs], qg_ref[...])
    for kv in range(N_KV):
        cs = slice(kv * HEAD_DIM, (kv + 1) * HEAD_DIM)
        k_ref[:, cs] = _head_norm(qkv[:, Q_DIM + kv * HEAD_DIM:Q_DIM + (kv + 1) * HEAD_DIM], kg_ref[...])
    v_ref[...] = qkv[:, Q_DIM + KV_DIM:]
    kvt = _dot_nt(wkvt_ref[...], xb)
    for kv in range(N_KV):
        rs = slice(kv * HEAD_DIM, (kv + 1) * HEAD_DIM)
        z = kvt[rs, :]
        kt_ref[rs, :] = z * lax.rsqrt(jnp.mean(z * z, axis=0, keepdims=True) + RMS_EPS) * kgt_ref[...]
    vt_ref[...] = kvt[KV_DIM:, :]


def _swa_sample_qkv(x, g, wqkv, wkvt, qg, kg, kgt):
    b = x.shape[0]
    shapes = [(b, D_MODEL), (1, D_MODEL), (D_MODEL, QKV_DIM), (2 * KV_DIM, D_MODEL), (1, HEAD_DIM), (1, HEAD_DIM),
              (HEAD_DIM, b)]
    out_shapes = [(b, Q_DIM), (b, KV_DIM), (b, KV_DIM), (KV_DIM, b), (KV_DIM, b)]
    return pl.pallas_call(
        _swa_sample_qkv_kernel,
        grid=(1,),
        in_specs=[_const_spec(s) for s in shapes],
        out_specs=[pl.BlockSpec(s, lambda i: (0, 0)) for s in out_shapes],
        out_shape=[jax.ShapeDtypeStruct(s, F32) for s in out_shapes],
        compiler_params=_params(1),
        name="swa_sample_qkv",
    )(x, g, wqkv, wkvt, qg, kg, kgt)


def _swa_sample_attn_kernel(q_ref, kn_ref, vn_ref, knt_ref, vnt_ref, kc_ref, vc_ref, sink_ref, slope_ref,
                            o_ref, ko_ref, vo_ref):
    bs = q_ref.shape[0]
    w = kc_ref.shape[3]
    i = pl.program_id(0)
    hrow = lax.broadcasted_iota(jnp.int32, (N_HEADS, KV_DIM), 0)
    feat = lax.broadcasted_iota(jnp.int32, (N_HEADS, KV_DIM), 1)
    own = (hrow // GROUP) == (feat // HEAD_DIM)
    key = lax.broadcasted_iota(jnp.int32, (N_HEADS, w), 1)
    dist = w - key
    bias = jnp.where(dist < WINDOW, -(slope_ref[:, 0:1] * dist.astype(F32)), NEG_INF)
    sink = sink_ref[:, 0:1]
    lane = lax.broadcasted_iota(jnp.int32, (KV_DIM, w), 1)
    newest = lane == w - 1

    def scores(b):
        q = q_ref[b]
        qt = jnp.concatenate([q] * N_KV, axis=1)
        qm = (jnp.where(own, qt, 0.0) * (HEAD_DIM ** -0.5)).astype(BF16)
        return qm, _dot(qm, kc_ref[0, b].astype(BF16)) + bias

    def softmax(b, qm, s):
        kn = kn_ref[b].astype(BF16).astype(F32)
        s_new = jnp.sum(qm.astype(F32) * kn, axis=-1, keepdims=True)
        m = jnp.maximum(jnp.maximum(jnp.max(s, axis=-1, keepdims=True), s_new), sink)
        e = jnp.exp(s - m)
        e_new = jnp.exp(s_new - m)
        den = jnp.sum(e, axis=-1, keepdims=True) + e_new + jnp.exp(sink - m)
        return (e / den).astype(BF16), (e_new / den).astype(BF16).astype(F32)

    def values(b, p, p_new):
        vn = vn_ref[b].astype(BF16).astype(F32)
        return _dot_nt(p, vc_ref[0, b].astype(BF16)) + p_new * vn

    def emit(b, res):
        res = jnp.where(own, res, 0.0)
        o = res[:, 0:HEAD_DIM]
        for kv in range(1, N_KV):
            o = o + res[:, kv * HEAD_DIM:(kv + 1) * HEAD_DIM]
        o_ref[b] = o

    def shift_in(b):
        mine = lane == i * bs + b
        kcol = jnp.sum(jnp.where(mine, knt_ref[...], 0.0), axis=1, keepdims=True)
        vcol = jnp.sum(jnp.where(mine, vnt_ref[...], 0.0), axis=1, keepdims=True)
        ko_ref[b] = jnp.where(newest, kcol, pltpu.roll(kc_ref[0, b], w - 1, 1))
        vo_ref[b] = jnp.where(newest, vcol, pltpu.roll(vc_ref[0, b], w - 1, 1))

    def body(grp, carry):
        seqs = [grp * SWA_SAMPLE_UNROLL + u for u in range(SWA_SAMPLE_UNROLL)]
        staged = [scores(b) for b in seqs]
        probs = [softmax(b, qm, s) for b, (qm, s) in zip(seqs, staged)]
        for b in seqs:
            shift_in(b)
        outs = [values(b, p, p_new) for b, (p, p_new) in zip(seqs, probs)]
        for b, res in zip(seqs, outs):
            emit(b, res)
        return carry

    lax.fori_loop(0, bs // SWA_SAMPLE_UNROLL, body, 0)


def _swa_sample_attn(layer, q3, kn3, vn3, knt, vnt, kct, vct, sink_b, slope_b):
    _, b, _, w = kct.shape
    assert w == 128 and b == 128, "one lane tile of key positions; new-token columns indexed by lane"
    bs = SWA_SAMPLE_BS
    cache_spec = pl.BlockSpec((1, bs, KV_DIM, w), lambda i: (layer, i, 0, 0))
    return pl.pallas_call(
        _swa_sample_attn_kernel,
        grid=(b // bs,),
        in_specs=[
            pl.BlockSpec((bs, N_HEADS, HEAD_DIM), lambda i: (i, 0, 0)),
            pl.BlockSpec((bs, 1, KV_DIM), lambda i: (i, 0, 0)),
            pl.BlockSpec((bs, 1, KV_DIM), lambda i: (i, 0, 0)),
            _const_spec((KV_DIM, b)),
            _const_spec((KV_DIM, b)),
            cache_spec,
            cache_spec,
            _const_spec((N_HEADS, 128)),
            _const_spec((N_HEADS, 128)),
        ],
        out_specs=[
            pl.BlockSpec((bs, N_HEADS, HEAD_DIM), lambda i: (i, 0, 0)),
            pl.BlockSpec((bs, KV_DIM, w), lambda i: (i, 0, 0)),
            pl.BlockSpec((bs, KV_DIM, w), lambda i: (i, 0, 0)),
        ],
        out_shape=[
            jax.ShapeDtypeStruct((b, N_HEADS, HEAD_DIM), F32),
            jax.ShapeDtypeStruct((b, KV_DIM, w), F32),
            jax.ShapeDtypeStruct((b, KV_DIM, w), F32),
        ],
        compiler_params=_params(1),
        name="swa_sample_attn",
    )(q3, kn3, vn3, knt, vnt, kct, vct, sink_b, slope_b)


def _proj_residual_kernel(x_ref, o_ref_in, wout_ref, y_ref):
    y_ref[...] = x_ref[...] + _dot(o_ref_in[...].astype(BF16), wout_ref[...])


def _proj_residual(x, o, wout):
    b = x.shape[0]
    return pl.pallas_call(
        _proj_residual_kernel,
        grid=(1,),
        in_specs=[_const_spec((b, D_MODEL)), _const_spec((b, Q_DIM)), _const_spec((Q_DIM, D_MODEL))],
        out_specs=pl.BlockSpec((b, D_MODEL), lambda i: (0, 0)),
        out_shape=jax.ShapeDtypeStruct((b, D_MODEL), F32),
        compiler_params=_params(1),
        name="proj_residual",
    )(x, o, wout)


def kernel(x_prompt, x_sample, state_rglru_h, state_rglru_conv, cache_swa_k, cache_swa_v, norm_mix_g, norm_mlp_g, lru_w_in, lru_conv_w, lru_conv_b, lru_w_a, lru_b_a, lru_w_x, lru_b_x, lru_lambda, lru_w_out, attn_w_qkv, attn_q_norm, attn_k_norm, attn_sinks, attn_w_out, mlp_w_up, mlp_w_down):
    n_p, t_p, _ = x_prompt.shape
    n_s = x_sample.shape[0]
    w_buf = cache_swa_k.shape[2]
    heads = jnp.arange(1, N_HEADS + 1, dtype=F32)
    slopes = jnp.exp2(-8.0 * heads / N_HEADS)
    slopes_b = jnp.broadcast_to(slopes[:, None], (N_HEADS, 128))
    head_of_lane = jnp.arange(KV_DIM) // HEAD_DIM
    head_ones = (head_of_lane[:, None] == head_of_lane[None, :]).astype(BF16)

    to_feature_major = lambda c: c.transpose(0, 1, 3, 4, 2).reshape(c.shape[0], n_s, KV_DIM, w_buf)
    from_feature_major = lambda c: c.reshape(c.shape[0], n_s, N_KV, HEAD_DIM, w_buf).transpose(0, 1, 4, 2, 3)
    kct, vct = to_feature_major(cache_swa_k), to_feature_major(cache_swa_v)

    wax_f32 = jnp.concatenate([lru_w_a, lru_w_x], axis=-1).reshape(N_A, D_RNN, 2 * LRU_BW)
    wkvt = attn_w_qkv[:, :, Q_DIM:].transpose(0, 2, 1).astype(BF16)

    def weight_sources(layer):
        mixer = ([(lru_w_in, layer // 2), (wax_f32, layer // 2), (lru_w_out, layer // 2)] if layer % 2 == 0 else
                 [(attn_w_qkv, layer // 2), (attn_w_out, layer // 2)])
        return mixer + [(mlp_w_up, layer), (mlp_w_down, layer)]

    weights = [w[layer].astype(BF16) for w, layer in weight_sources(0)]

    row = lambda v: v.reshape(1, -1)
    yp = x_prompt
    ys = x_sample.reshape(n_s, D_MODEL)
    h_p, c_p, k_p, v_p, h_s, c_s, k_s, v_s = ([] for _ in range(8))
    for layer in range(DEPTH):
        j = layer // 2
        g_mix = row(norm_mix_g[layer])
        g_mlp = row(norm_mlp_g[layer])
        casts = weight_sources(layer + 1) if layer + 1 < DEPTH else []
        *mixer_w, wu, wd = weights
        if layer % 2 == 0:
            win, wax, wlo = mixer_w
            args = (g_mix, win, lru_conv_w[j], row(lru_conv_b[j]), wax.reshape(LRU_BLOCKS, LRU_BW, 2 * LRU_BW),
                    row(lru_b_a[j]), row(lru_b_x[j]), row(lru_lambda[j]), wlo)
            ys, hs, cs = _lru_sample(ys, state_rglru_h[j], state_rglru_conv[j].transpose(1, 0, 2), *args)
            yp, ys, hp, cp, *weights = _lru_mlp(yp, ys, *args, g_mlp, wu, wd, casts)
            h_p.append(hp.reshape(n_p, D_RNN)); c_p.append(cp)
            h_s.append(hs); c_s.append(cs.transpose(1, 0, 2))
        else:
            wqkv, wao = mixer_w
            qg, kg = row(attn_q_norm[j]), row(attn_k_norm[j])
            sinks = attn_sinks[j]
            kgt = jnp.broadcast_to(attn_k_norm[j][:, None], (HEAD_DIM, n_s))
            q, kn, vn, knt, vnt = _swa_sample_qkv(ys, g_mix, wqkv, wkvt[j], qg, kg, kgt)
            o3, ks, vs = _swa_sample_attn(
                j, q.reshape(n_s, N_HEADS, HEAD_DIM), kn.reshape(n_s, 1, KV_DIM), vn.reshape(n_s, 1, KV_DIM),
                knt, vnt, kct, vct, jnp.broadcast_to(sinks[:, None], (N_HEADS, 128)), slopes_b)
            k_s.append(ks); v_s.append(vs)
            ys = _proj_residual(ys, o3.reshape(n_s, Q_DIM), wao)
            stack = (k_s, v_s) if len(k_s) == N_B else ()
            yp, ys, kp, vp, *rest = _swa_mlp(
                yp, ys, sinks, slopes, g_mix, wqkv, jnp.tile(qg, (1, GROUP)), jnp.tile(kg, (1, N_KV)),
                head_ones, wao, g_mlp, wu, wd, stack, casts)
            weights, stacked = rest[:len(casts)], rest[len(casts):]
            k_p.append(kp.reshape(n_p, w_buf, N_KV, HEAD_DIM)); v_p.append(vp.reshape(n_p, w_buf, N_KV, HEAD_DIM))
    return (yp, ys.reshape(n_s, 1, D_MODEL),
            jnp.stack(h_p), jnp.stack(c_p), jnp.stack(k_p), jnp.stack(v_p),
            jnp.stack(h_s), jnp.stack(c_s), from_feature_major(stacked[0]), from_feature_major(stacked[1]))
```

```python
import functools

import jax
import jax.numpy as jnp
from jax import lax
from jax.experimental import pallas as pl
from jax.experimental.pallas import tpu as pltpu

F32 = jnp.float32
BF16 = jnp.bfloat16

D_MODEL = 1024
D_RNN = 1024
D_FF = 4096
DEPTH = 4
N_A = (DEPTH + 1) // 2
N_B = DEPTH // 2
CONV_W = 4
LRU_BLOCKS = 4
LRU_BW = D_RNN // LRU_BLOCKS
LRU_C = 8.0
N_HEADS = 16
N_KV = 4
GROUP = N_HEADS // N_KV
HEAD_DIM = 64
Q_DIM = N_HEADS * HEAD_DIM
KV_DIM = N_KV * HEAD_DIM
QKV_DIM = Q_DIM + 2 * KV_DIM
WINDOW = 128
assert GROUP * HEAD_DIM == KV_DIM and N_KV % 2 == 0 and GROUP % 2 == 0
RMS_EPS = 1e-6
NEG_INF = -1e30

V7X_VMEM_LIMIT_BYTES = 56 * 1024 * 1024
SUBLANES = 8
BF16_SUBLANES = 16
CAST_BLOCKS = 64
STACK_BLOCKS = 64

MLP_FC = 1024
MLP_LEAD = 3
LRU_TM = 256
SWA_TM = 256
SWA_SAMPLE_BS = 16
SWA_SAMPLE_UNROLL = 8


def _params(n_axes):
    return pltpu.CompilerParams(
        dimension_semantics=("arbitrary",) * n_axes,
        vmem_limit_bytes=V7X_VMEM_LIMIT_BYTES,
    )


def _const_spec(shape):
    nd = len(shape)
    return pl.BlockSpec(shape, lambda *_: (0,) * nd, pipeline_mode=pl.Buffered(1))


def _rms(x, g):
    return x * lax.rsqrt(jnp.mean(x * x, axis=-1, keepdims=True) + RMS_EPS) * g


def _dot(a, b):
    return jnp.dot(a, b, preferred_element_type=F32)


def _dot_nt(a, b):
    return lax.dot_general(a, b, (((1,), (1,)), ((), ())), preferred_element_type=F32)


def _cast_specs(sources):
    block = lambda i: jnp.minimum(i, CAST_BLOCKS - 1)
    in_specs, out_specs, out_shapes = [], [], []
    for w, layer in sources:
        _, r, c = w.shape
        rb = r // CAST_BLOCKS
        assert rb * CAST_BLOCKS == r and rb % BF16_SUBLANES == 0
        in_specs.append(pl.BlockSpec((None, rb, c), lambda i, layer=layer: (layer, block(i), 0)))
        out_specs.append(pl.BlockSpec((rb, c), lambda i: (block(i), 0)))
        out_shapes.append(jax.ShapeDtypeStruct((r, c), BF16))
    return in_specs, out_specs, out_shapes


def _stack_specs(groups):
    block = lambda i: jnp.minimum(i, STACK_BLOCKS - 1)
    in_specs, out_specs, out_shapes = [], [], []
    for group in groups:
        shape, n = group[0].shape, len(group)
        rb, rest = shape[0] // STACK_BLOCKS, shape[1:]
        assert rb * STACK_BLOCKS == shape[0]
        zeros = (0,) * len(rest)
        in_specs += [pl.BlockSpec((rb,) + rest, lambda i: (block(i),) + zeros)] * n
        out_specs.append(pl.BlockSpec((n, rb) + rest, lambda i: (0, block(i)) + zeros))
        out_shapes.append(jax.ShapeDtypeStruct((n,) + shape, group[0].dtype))
    return in_specs, out_specs, out_shapes


def _with_side_jobs(kernel, n_in, n_out, n_cast, group_sizes, *refs):
    n_parts = sum(group_sizes)
    ins, rest = refs[:n_in], refs[n_in:]
    src, parts, rest = rest[:n_cast], rest[n_cast:n_cast + n_parts], rest[n_cast + n_parts:]
    outs, dst, stacked = rest[:n_out], rest[n_out:n_out + n_cast], rest[n_out + n_cast:n_out + n_cast + len(group_sizes)]
    kernel(*ins, *outs, *rest[n_out + n_cast + len(group_sizes):])
    for s_ref, d_ref in zip(src, dst):
        d_ref[...] = s_ref[...].astype(BF16)
    for d_ref, size in zip(stacked, group_sizes):
        for l in range(size):
            d_ref[l] = parts[l][...]
        parts = parts[size:]


def _softplus(x):
    return jnp.maximum(x, 0.0) + jnp.log1p(jnp.exp(-jnp.abs(x)))


def _lru_gates(xc_k, ga, ba_k, bx_k, sp_k):
    r = jax.nn.sigmoid(ga[:, :LRU_BW] + ba_k)
    i = jax.nn.sigmoid(ga[:, LRU_BW:] + bx_k)
    log_a = -LRU_C * r * sp_k
    a = jnp.exp(log_a)
    u = jnp.sqrt(-jnp.tanh(log_a) * (a * a + 1.0)) * (i * xc_k)
    return a, u


def _scan_rows(a, u, h0):
    rows, c = a.shape
    groups = rows // SUBLANES
    a = a.reshape(groups, SUBLANES, c)
    u = u.reshape(groups, SUBLANES, c)
    row = lax.broadcasted_iota(jnp.int32, a.shape, 1)
    d = 1
    while d < SUBLANES:
        keep = row >= d
        u = jnp.where(keep, u + a * pltpu.roll(u, d, 1), u)
        a = jnp.where(keep, a * pltpu.roll(a, d, 1), a)
        d *= 2
    hs = []
    h = h0
    for g in range(groups):
        hg = a[g] * h + u[g]
        hs.append(hg)
        h = hg[SUBLANES - 1:SUBLANES, :]
    return jnp.concatenate(hs, axis=0)


def _lru_mlp_kernel(n_seq, x_ref, xs_ref, g_ref, win_ref, cw_ref, cb_ref, wax_ref, ba_ref, bx_ref, lam_ref,
                    wout_ref, g2_ref, wu_ref, wd_ref, o_ref, os_ref, h_ref, c_ref, xbuf, hcar, ymid):
    tm = x_ref.shape[1]
    nj = (pl.num_programs(0) - 1) // n_seq - 1
    i = pl.program_id(0)
    j = lax.rem(i, nj + 1)

    def mlp_chunk(xb, c):
        h = _dot(xb, wu_ref[:, c * MLP_FC:(c + 1) * MLP_FC])
        return jnp.square(jnp.maximum(h, 0.0)).astype(BF16)

    def step(do_mixer, do_mlp):
        if do_mlp:
            yprev = ymid[...]
            yb = _rms(yprev, g2_ref[...]).astype(BF16)
            mlp = yprev
        if not do_mixer:
            for c in range(D_FF // MLP_FC):
                mlp = mlp + _dot(mlp_chunk(yb, c), wd_ref[c * MLP_FC:(c + 1) * MLP_FC, :])
            o_ref[0] = mlp
            return
        x = x_ref[0]
        xb = _rms(x, g_ref[...]).astype(BF16)
        u_in = _dot(xb, win_ref[...])
        hidden = {c: mlp_chunk(yb, c) for c in range(MLP_LEAD)} if do_mlp else {}
        xr = u_in[:, D_RNN:]
        xbuf[SUBLANES:SUBLANES + tm, :] = xr
        cw = cw_ref[...]
        xc = xbuf[pl.ds(SUBLANES - 3, tm), :] * cw[0:1, :]
        xc = xc + xbuf[pl.ds(SUBLANES - 2, tm), :] * cw[1:2, :]
        xc = xc + xbuf[pl.ds(SUBLANES - 1, tm), :] * cw[2:3, :]
        xc = xc + xr * cw[3:4, :]
        xc = xc + cb_ref[...]
        xbuf[0:SUBLANES, :] = xbuf[tm:tm + SUBLANES, :]

        sp = _softplus(-lam_ref[...])
        blocks = [slice(k * LRU_BW, (k + 1) * LRU_BW) for k in range(LRU_BLOCKS)]
        gates = [_dot(xc[:, cs].astype(BF16), wax_ref[k]) for k, cs in enumerate(blocks)]
        acc = x
        for k, cs in enumerate(blocks):
            if do_mlp and k + MLP_LEAD < LRU_BLOCKS:
                hidden[k + MLP_LEAD] = mlp_chunk(yb, k + MLP_LEAD)
            a, u = _lru_gates(xc[:, cs], gates[k], ba_ref[:, cs], bx_ref[:, cs], sp[:, cs])
            h = _scan_rows(a, u, hcar[:, cs])
            hcar[:, cs] = h[tm - 1:tm, :]
            y = h * jax.nn.gelu(u_in[:, cs])
            if do_mlp:
                mlp = mlp + _dot(hidden.pop(k), wd_ref[k * MLP_FC:(k + 1) * MLP_FC, :])
            acc = acc + _dot(y.astype(BF16), wout_ref[cs, :])
        ymid[...] = acc
        if do_mlp:
            o_ref[0] = mlp

    prompt = i < n_seq * (nj + 1)

    @pl.when(prompt & (j == 0))
    def _():
        xbuf[0:SUBLANES, :] = jnp.zeros((SUBLANES, D_RNN), F32)
        hcar[...] = jnp.zeros_like(hcar)
        step(True, False)

    pl.when(prompt & (j > 0) & (j < nj))(lambda: step(True, True))
    pl.when(prompt & (j == nj))(lambda: step(False, True))

    @pl.when(prompt & (j == nj - 1))
    def _():
        h_ref[0] = hcar[...]
        c_ref[0] = xbuf[pl.ds(SUBLANES - (CONV_W - 1), CONV_W - 1), :]

    @pl.when(i == n_seq * (nj + 1))
    def _():
        x = xs_ref[...]
        xb = _rms(x, g2_ref[...]).astype(BF16)
        acc = x
        for c in range(D_FF // MLP_FC):
            acc = acc + _dot(mlp_chunk(xb, c), wd_ref[c * MLP_FC:(c + 1) * MLP_FC, :])
        os_ref[...] = acc


def _lru_mlp(x, xs, g, win, cw, cb, wax, ba, bx, lam, wout, g2, wu, wd, casts=()):
    n, t, _ = x.shape
    rows_s = xs.shape[0]
    tm = LRU_TM
    assert D_FF // MLP_FC == LRU_BLOCKS
    nj = t // tm
    steps = nj + 1
    seq = lambda i: jnp.minimum(i // steps, n - 1)
    chunk_in = lambda i: jnp.where(i >= n * steps, nj - 1, jnp.minimum(lax.rem(i, steps), nj - 1))
    chunk_out = lambda i: jnp.where(i >= n * steps, nj - 1, jnp.maximum(lax.rem(i, steps) - 1, 0))
    sample_spec = pl.BlockSpec((rows_s, D_MODEL), lambda i: (0, 0))
    assert n * steps + 1 >= CAST_BLOCKS
    cast_in, cast_out, cast_shapes = _cast_specs(casts)
    return pl.pallas_call(
        functools.partial(_with_side_jobs, functools.partial(_lru_mlp_kernel, n), 14, 4, len(casts), ()),
        grid=(n * steps + 1,),
        in_specs=[
            pl.BlockSpec((1, tm, D_MODEL), lambda i: (seq(i), chunk_in(i), 0)),
            sample_spec,
            _const_spec((1, D_MODEL)),
            _const_spec((D_MODEL, 2 * D_RNN)),
            _const_spec((CONV_W, D_RNN)),
            _const_spec((1, D_RNN)),
            _const_spec((LRU_BLOCKS, LRU_BW, 2 * LRU_BW)),
            _const_spec((1, D_RNN)),
            _const_spec((1, D_RNN)),
            _const_spec((1, D_RNN)),
            _const_spec((D_RNN, D_MODEL)),
            _const_spec((1, D_MODEL)),
            _const_spec((D_MODEL, D_FF)),
            _const_spec((D_FF, D_MODEL)),
        ] + cast_in,
        out_specs=[
            pl.BlockSpec((1, tm, D_MODEL), lambda i: (seq(i), chunk_out(i), 0)),
            sample_spec,
            pl.BlockSpec((1, 1, D_RNN), lambda i: (seq(i), 0, 0)),
            pl.BlockSpec((1, CONV_W - 1, D_RNN), lambda i: (seq(i), 0, 0)),
        ] + cast_out,
        out_shape=[
            jax.ShapeDtypeStruct((n, t, D_MODEL), F32),
            jax.ShapeDtypeStruct((rows_s, D_MODEL), F32),
            jax.ShapeDtypeStruct((n, 1, D_RNN), F32),
            jax.ShapeDtypeStruct((n, CONV_W - 1, D_RNN), F32),
        ] + cast_shapes,
        scratch_shapes=[
            pltpu.VMEM((SUBLANES + tm, D_RNN), F32),
            pltpu.VMEM((1, D_RNN), F32),
            pltpu.VMEM((tm, D_MODEL), F32),
        ],
        compiler_params=_params(1),
        name="lru_mlp",
    )(x, xs, g, win, cw, cb, wax, ba, bx, lam, wout, g2, wu, wd, *[w for w, _ in casts])


def _lru_sample_kernel(x_ref, h0_ref, c0_ref, g_ref, win_ref, cw_ref, cb_ref, wax_ref, ba_ref, bx_ref, lam_ref,
                       wout_ref, o_ref, h_ref, c_ref):
    x = x_ref[...]
    xb = _rms(x, g_ref[...]).astype(BF16)
    u_in = _dot(xb, win_ref[...])
    xr = u_in[:, D_RNN:]
    cw = cw_ref[...]
    xc = c0_ref[0] * cw[0:1, :]
    xc = xc + c0_ref[1] * cw[1:2, :]
    xc = xc + c0_ref[2] * cw[2:3, :]
    xc = xc + xr * cw[3:4, :]
    xc = xc + cb_ref[...]
    c_ref[0] = c0_ref[1]
    c_ref[1] = c0_ref[2]
    c_ref[2] = xr

    sp = _softplus(-lam_ref[...])
    acc = x
    for k in range(LRU_BLOCKS):
        cs = slice(k * LRU_BW, (k + 1) * LRU_BW)
        ga = _dot(xc[:, cs].astype(BF16), wax_ref[k])
        a, u = _lru_gates(xc[:, cs], ga, ba_ref[:, cs], bx_ref[:, cs], sp[:, cs])
        h = a * h0_ref[:, cs] + u
        h_ref[:, cs] = h
        y = h * jax.nn.gelu(u_in[:, cs])
        acc = acc + _dot(y.astype(BF16), wout_ref[cs, :])
    o_ref[...] = acc


def _lru_sample(x, h0, c0, g, win, cw, cb, wax, ba, bx, lam, wout):
    b = x.shape[0]
    shapes = [(b, D_MODEL), (b, D_RNN), (CONV_W - 1, b, D_RNN), (1, D_MODEL), (D_MODEL, 2 * D_RNN),
              (CONV_W, D_RNN), (1, D_RNN), (LRU_BLOCKS, LRU_BW, 2 * LRU_BW), (1, D_RNN), (1, D_RNN), (1, D_RNN),
              (D_RNN, D_MODEL)]
    return pl.pallas_call(
        _lru_sample_kernel,
        grid=(1,),
        in_specs=[_const_spec(s) for s in shapes],
        out_specs=[
            pl.BlockSpec((b, D_MODEL), lambda i: (0, 0)),
            pl.BlockSpec((b, D_RNN), lambda i: (0, 0)),
            pl.BlockSpec((CONV_W - 1, b, D_RNN), lambda i: (0, 0, 0)),
        ],
        out_shape=[
            jax.ShapeDtypeStruct((b, D_MODEL), F32),
            jax.ShapeDtypeStruct((b, D_RNN), F32),
            jax.ShapeDtypeStruct((CONV_W - 1, b, D_RNN), F32),
        ],
        compiler_params=_params(1),
        name="lru_sample",
    )(x, h0, c0, g, win, cw, cb, wax, ba, bx, lam, wout)


def _head_norm(z, g):
    return z * lax.rsqrt(jnp.mean(z * z, axis=-1, keepdims=True) + RMS_EPS) * g


def _heads_norm(z, g):
    z2 = z * z
    low = lax.broadcasted_iota(jnp.int32, (z.shape[0], 2 * HEAD_DIM), 1) < HEAD_DIM
    parts = []
    for c in range(z.shape[1] // (2 * HEAD_DIM)):
        zz = z2[:, c * 2 * HEAD_DIM:(c + 1) * 2 * HEAD_DIM]
        s_lo = jnp.sum(jnp.where(low, zz, 0.0), axis=-1, keepdims=True)
        s_hi = jnp.sum(jnp.where(low, 0.0, zz), axis=-1, keepdims=True)
        parts.append(jnp.where(low, s_lo, s_hi))
    ssq = jnp.concatenate(parts, axis=1)
    return z * lax.rsqrt(ssq * (1.0 / HEAD_DIM) + RMS_EPS) * g


_MLP_ORDER = (("up", 0), ("up", 1), ("down", 0), ("up", 2), ("down", 1), ("up", 3), ("down", 2), ("down", 3))


def _swa_mlp_kernel(n_seq, sink_ref, slope_ref, x_ref, xs_ref, g_ref, wqkv_ref, qg_ref, kg_ref,
                    wout_ref, g2_ref, wu_ref, wd_ref, o_ref, os_ref, k_ref, v_ref, k2, v2, qn, obuf, bias, ymid):
    i = pl.program_id(0)
    nj = (pl.num_programs(0) - 1) // n_seq - 1

    def mlp_up(xb, c):
        h = _dot(xb, wu_ref[:, c * MLP_FC:(c + 1) * MLP_FC])
        return jnp.square(jnp.maximum(h, 0.0)).astype(BF16)

    def mlp_down(hk, c):
        return _dot(hk, wd_ref[c * MLP_FC:(c + 1) * MLP_FC, :])

    step = functools.partial(_swa_mlp_step, i == 0, sink_ref, slope_ref, x_ref, g_ref, wqkv_ref, qg_ref, kg_ref,
                             wout_ref, g2_ref, mlp_up, mlp_down, o_ref, k_ref, v_ref, k2, v2, qn, obuf,
                             bias, ymid)
    prompt = i < n_seq * (nj + 1)
    j = lax.rem(i, nj + 1)
    pl.when(prompt & (j == 0))(lambda: step(True, False))
    pl.when(prompt & (j > 0) & (j < nj))(lambda: step(True, True))
    pl.when(prompt & (j == nj))(lambda: step(False, True))

    @pl.when(i == n_seq * (nj + 1))
    def _():
        x = xs_ref[...]
        xb = _rms(x, g2_ref[...]).astype(BF16)
        acc = x
        for c in range(D_FF // MLP_FC):
            acc = acc + mlp_down(mlp_up(xb, c), c)
        os_ref[...] = acc


def _swa_mlp_step(first, sink_ref, slope_ref, x_ref, g_ref, wqkv_ref, qg_ref, kg_ref, wout_ref,
                  g2_ref, mlp_up, mlp_down, o_ref, k_ref, v_ref, k2, v2, qn, obuf, bias, ymid, do_mixer, do_mlp):
    tm = x_ref.shape[1]
    nblk = tm // WINDOW
    first_chunk = not do_mlp

    if do_mlp:
        yprev = ymid[...]
        yb = _rms(yprev, g2_ref[...]).astype(BF16)
        mlp = yprev
    if not do_mixer:
        for c in range(D_FF // MLP_FC):
            mlp = mlp + mlp_down(mlp_up(yb, c), c)
        o_ref[0] = mlp
        return

    if first_chunk:
        @pl.when(first)
        def _():
            qi = lax.broadcasted_iota(jnp.int32, (WINDOW, 2 * WINDOW), 0)
            si = lax.broadcasted_iota(jnp.int32, (WINDOW, 2 * WINDOW), 1)
            dist = qi + WINDOW - si
            valid = (dist >= 0) & (dist < WINDOW)
            dist_f = dist.astype(F32)
            for h in range(N_HEADS):
                bias[h] = jnp.where(valid, -(slope_ref[h] * dist_f), NEG_INF)

        k2[:, :, 0:WINDOW, :] = jnp.zeros((N_KV, 2, WINDOW, 2 * HEAD_DIM), BF16)
        v2[:, :, 0:WINDOW, :] = jnp.zeros((N_KV, 2, WINDOW, 2 * HEAD_DIM), BF16)

    x = x_ref[0]
    xb = _rms(x, g_ref[...]).astype(BF16)
    qkv = _dot(xb, wqkv_ref[...])

    kn = _heads_norm(qkv[:, Q_DIM:Q_DIM + KV_DIM], kg_ref[...])
    vv = qkv[:, Q_DIM + KV_DIM:]
    k_ref[0] = kn[tm - WINDOW:, :]
    v_ref[0] = vv[tm - WINDOW:, :]
    low = lax.broadcasted_iota(jnp.int32, (tm, 2 * HEAD_DIM), 1) < HEAD_DIM
    for src, dst in ((kn, k2), (vv, v2)):
        for sl in range(N_KV // 2):
            z = src[:, sl * 2 * HEAD_DIM:(sl + 1) * 2 * HEAD_DIM]
            zs = pltpu.roll(z, HEAD_DIM, 1)
            dst[2 * sl, 0, WINDOW:WINDOW + tm, :] = jnp.where(low, z, 0.0).astype(BF16)
            dst[2 * sl, 1, WINDOW:WINDOW + tm, :] = jnp.where(low, 0.0, zs).astype(BF16)
            dst[2 * sl + 1, 0, WINDOW:WINDOW + tm, :] = jnp.where(low, zs, 0.0).astype(BF16)
            dst[2 * sl + 1, 1, WINDOW:WINDOW + tm, :] = jnp.where(low, 0.0, z).astype(BF16)
    qg = qg_ref[...] * (HEAD_DIM ** -0.5)
    for kv in range(N_KV):
        cs = slice(kv * KV_DIM, (kv + 1) * KV_DIM)
        qn[:, cs] = _heads_norm(qkv[:, cs], qg).astype(BF16)

    col = lax.broadcasted_iota(jnp.int32, (1, 2 * WINDOW), 1)
    no_prev = jnp.where(col < WINDOW, NEG_INF, 0.0)

    def scores(kv, i):
        rq = slice(i * WINDOW, (i + 1) * WINDOW)
        ql = jnp.concatenate([qn[rq, kv * KV_DIM + p * 2 * HEAD_DIM:kv * KV_DIM + (p + 1) * 2 * HEAD_DIM]
                              for p in range(GROUP // 2)], axis=0)
        return [_dot_nt(ql, k2[kv, c, i * WINDOW:(i + 2) * WINDOW, :]) for c in range(2)]

    work = [(kv, i) for kv in range(N_KV) for i in range(nblk)]
    hidden = {}
    s_next = scores(*work[0])
    for n, (kv, i) in enumerate(work):
        mlp_ops = _MLP_ORDER[n * len(_MLP_ORDER) // len(work):(n + 1) * len(_MLP_ORDER) // len(work)]
        for kind, c in mlp_ops if do_mlp else ():
            if kind == "up":
                hidden[c] = mlp_up(yb, c)
            else:
                mlp = mlp + mlp_down(hidden.pop(c), c)
        rq = slice(i * WINDOW, (i + 1) * WINDOW)
        rk = slice(i * WINDOW, (i + 2) * WINDOW)
        s = s_next
        if n + 1 < len(work):
            s_next = scores(*work[n + 1])
        probs = [[None, None], [None, None]]
        for g in range(GROUP):
            p, c = divmod(g, 2)
            h = kv * GROUP + g
            sink = sink_ref[h]
            sg = s[c][p * WINDOW:(p + 1) * WINDOW, :] + bias[h]
            if i == 0 and first_chunk:
                sg = sg + no_prev
            m = jnp.maximum(jnp.max(sg, axis=-1, keepdims=True), sink)
            e = jnp.exp(sg - m)
            pr = e / (jnp.sum(e, axis=-1, keepdims=True) + jnp.exp(sink - m))
            probs[c][p] = pr.astype(BF16)
        pv = (_dot(jnp.concatenate(probs[0], axis=0), v2[kv, 0, rk, :])
              + _dot(jnp.concatenate(probs[1], axis=0), v2[kv, 1, rk, :]))
        for p in range(GROUP // 2):
            obuf[rq, kv * KV_DIM + p * 2 * HEAD_DIM:kv * KV_DIM + (p + 1) * 2 * HEAD_DIM] = (
                pv[p * WINDOW:(p + 1) * WINDOW, :].astype(BF16))

    assert not hidden
    k2[:, :, 0:WINDOW, :] = k2[:, :, tm:tm + WINDOW, :]
    v2[:, :, 0:WINDOW, :] = v2[:, :, tm:tm + WINDOW, :]
    ymid[...] = x + _dot(obuf[...], wout_ref[...])
    if do_mlp:
        o_ref[0] = mlp


def _swa_mlp(x, xs, sinks, slopes, g, wqkv, qg4, kg4, wout, g2, wu, wd, stack=(), casts=()):
    n, t, _ = x.shape
    rows_s = xs.shape[0]
    tm = SWA_TM
    nj = t // tm
    steps = nj + 1
    seq = lambda i: jnp.minimum(i // steps, n - 1)
    chunk_in = lambda i: jnp.where(i >= n * steps, nj - 1, jnp.minimum(lax.rem(i, steps), nj - 1))
    chunk_out = lambda i: jnp.where(i >= n * steps, nj - 1, jnp.maximum(lax.rem(i, steps) - 1, 0))
    sample_spec = pl.BlockSpec((rows_s, D_MODEL), lambda i: (0, 0))
    smem = pl.BlockSpec(memory_space=pltpu.SMEM)
    parts = [p for group in stack for p in group]
    assert n * steps + 1 >= max(CAST_BLOCKS, STACK_BLOCKS)
    cast_in, cast_out, cast_shapes = _cast_specs(casts)
    stack_in, stack_out, stack_shapes = _stack_specs(stack)
    return pl.pallas_call(
        functools.partial(_with_side_jobs, functools.partial(_swa_mlp_kernel, n), 12, 4, len(casts),
                          tuple(len(group) for group in stack)),
        grid=(n * steps + 1,),
        in_specs=[
            smem,
            smem,
            pl.BlockSpec((1, tm, D_MODEL), lambda i: (seq(i), chunk_in(i), 0)),
            sample_spec,
            _const_spec((1, D_MODEL)),
            _const_spec((D_MODEL, QKV_DIM)),
            _const_spec((1, KV_DIM)),
            _const_spec((1, KV_DIM)),
            _const_spec((Q_DIM, D_MODEL)),
            _const_spec((1, D_MODEL)),
            _const_spec((D_MODEL, D_FF)),
            _const_spec((D_FF, D_MODEL)),
        ] + cast_in + stack_in,
        out_specs=[
            pl.BlockSpec((1, tm, D_MODEL), lambda i: (seq(i), chunk_out(i), 0)),
            sample_spec,
            pl.BlockSpec((1, WINDOW, KV_DIM), lambda i: (seq(i), 0, 0)),
            pl.BlockSpec((1, WINDOW, KV_DIM), lambda i: (seq(i), 0, 0)),
        ] + cast_out + stack_out,
        out_shape=[
            jax.ShapeDtypeStruct((n, t, D_MODEL), F32),
            jax.ShapeDtypeStruct((rows_s, D_MODEL), F32),
            jax.ShapeDtypeStruct((n, WINDOW, KV_DIM), F32),
            jax.ShapeDtypeStruct((n, WINDOW, KV_DIM), F32),
        ] + cast_shapes + stack_shapes,
        scratch_shapes=[
            pltpu.VMEM((N_KV, 2, WINDOW + tm, 2 * HEAD_DIM), BF16),
            pltpu.VMEM((N_KV, 2, WINDOW + tm, 2 * HEAD_DIM), BF16),
            pltpu.VMEM((tm, Q_DIM), BF16),
            pltpu.VMEM((tm, Q_DIM), BF16),
            pltpu.VMEM((N_HEADS, WINDOW, 2 * WINDOW), F32),
            pltpu.VMEM((tm, D_MODEL), F32),
        ],
        compiler_params=_params(1),
        name="swa_mlp",
    )(sinks, slopes, x, xs, g, wqkv, qg4, kg4, wout, g2, wu, wd, *[w for w, _ in casts], *parts)


def _swa_sample_qkv_kernel(x_ref, g_ref, wqkv_ref, wkvt_ref, qg_ref, kg_ref, kgt_ref,
                           q_ref, k_ref, v_ref, kt_ref, vt_ref):
    xb = _rms(x_ref[...], g_ref[...]).astype(BF16)
    qkv = _dot(xb, wqkv_ref[...])
    for h in range(N_HEADS):
        cs = slice(h * HEAD_DIM, (h + 1) * HEAD_DIM)
        q_ref[:, cs] = _head_norm(qkv[:, cs], qg_ref[...])
    for kv in range(N_KV):
        cs = slice(kv * HEAD_DIM, (kv + 1) * HEAD_DIM)
        k_ref[:, cs] = _head_norm(qkv[:, Q_DIM + kv * HEAD_DIM:Q_DIM + (kv + 1) * HEAD_DIM], kg_ref[...])
    v_ref[...] = qkv[:, Q_DIM + KV_DIM:]
    kvt = _dot_nt(wkvt_ref[...], xb)
    for kv in range(N_KV):
        rs = slice(kv * HEAD_DIM, (kv + 1) * HEAD_DIM)
        z = kvt[rs, :]
        kt_ref[rs, :] = z * lax.rsqrt(jnp.mean(z * z, axis=0, keepdims=True) + RMS_EPS) * kgt_ref[...]
    vt_ref[...] = kvt[KV_DIM:, :]


def _swa_sample_qkv(x, g, wqkv, wkvt, qg, kg, kgt):
    b = x.shape[0]
    shapes = [(b, D_MODEL), (1, D_MODEL), (D_MODEL, QKV_DIM), (2 * KV_DIM, D_MODEL), (1, HEAD_DIM), (1, HEAD_DIM),
              (HEAD_DIM, b)]
    out_shapes = [(b, Q_DIM), (b, KV_DIM), (b, KV_DIM), (KV_DIM, b), (KV_DIM, b)]
    return pl.pallas_call(
        _swa_sample_qkv_kernel,
        grid=(1,),
        in_specs=[_const_spec(s) for s in shapes],
        out_specs=[pl.BlockSpec(s, lambda i: (0, 0)) for s in out_shapes],
        out_shape=[jax.ShapeDtypeStruct(s, F32) for s in out_shapes],
        compiler_params=_params(1),
        name="swa_sample_qkv",
    )(x, g, wqkv, wkvt, qg, kg, kgt)


def _swa_sample_attn_kernel(q_ref, kn_ref, vn_ref, knt_ref, vnt_ref, kc_ref, vc_ref, sink_ref, slope_ref,
                            o_ref, ko_ref, vo_ref):
    bs = q_ref.shape[0]
    w = kc_ref.shape[3]
    i = pl.program_id(0)
    hrow = lax.broadcasted_iota(jnp.int32, (N_HEADS, KV_DIM), 0)
    feat = lax.broadcasted_iota(jnp.int32, (N_HEADS, KV_DIM), 1)
    own = (hrow // GROUP) == (feat // HEAD_DIM)
    key = lax.broadcasted_iota(jnp.int32, (N_HEADS, w), 1)
    dist = w - key
    bias = jnp.where(dist < WINDOW, -(slope_ref[:, 0:1] * dist.astype(F32)), NEG_INF)
    sink = sink_ref[:, 0:1]
    lane = lax.broadcasted_iota(jnp.int32, (KV_DIM, w), 1)
    newest = lane == w - 1

    def scores(b):
        q = q_ref[b]
        qt = jnp.concatenate([q] * N_KV, axis=1)
        qm = (jnp.where(own, qt, 0.0) * (HEAD_DIM ** -0.5)).astype(BF16)
        return qm, _dot(qm, kc_ref[0, b].astype(BF16)) + bias

    def softmax(b, qm, s):
        kn = kn_ref[b].astype(BF16).astype(F32)
        s_new = jnp.sum(qm.astype(F32) * kn, axis=-1, keepdims=True)
        m = jnp.maximum(jnp.maximum(jnp.max(s, axis=-1, keepdims=True), s_new), sink)
        e = jnp.exp(s - m)
        e_new = jnp.exp(s_new - m)
        den = jnp.sum(e, axis=-1, keepdims=True) + e_new + jnp.exp(sink - m)
        return (e / den).astype(BF16), (e_new / den).astype(BF16).astype(F32)

    def values(b, p, p_new):
        vn = vn_ref[b].astype(BF16).astype(F32)
        return _dot_nt(p, vc_ref[0, b].astype(BF16)) + p_new * vn

    def emit(b, res):
        res = jnp.where(own, res, 0.0)
        o = res[:, 0:HEAD_DIM]
        for kv in range(1, N_KV):
            o = o + res[:, kv * HEAD_DIM:(kv + 1) * HEAD_DIM]
        o_ref[b] = o

    def shift_in(b):
        mine = lane == i * bs + b
        kcol = jnp.sum(jnp.where(mine, knt_ref[...], 0.0), axis=1, keepdims=True)
        vcol = jnp.sum(jnp.where(mine, vnt_ref[...], 0.0), axis=1, keepdims=True)
        ko_ref[b] = jnp.where(newest, kcol, pltpu.roll(kc_ref[0, b], w - 1, 1))
        vo_ref[b] = jnp.where(newest, vcol, pltpu.roll(vc_ref[0, b], w - 1, 1))

    def body(grp, carry):
        seqs = [grp * SWA_SAMPLE_UNROLL + u for u in range(SWA_SAMPLE_UNROLL)]
        staged = [scores(b) for b in seqs]
        probs = [softmax(b, qm, s) for b, (qm, s) in zip(seqs, staged)]
        for b in seqs:
            shift_in(b)
        outs = [values(b, p, p_new) for b, (p, p_new) in zip(seqs, probs)]
        for b, res in zip(seqs, outs):
            emit(b, res)
        return carry

    lax.fori_loop(0, bs // SWA_SAMPLE_UNROLL, body, 0)


def _swa_sample_attn(layer, q3, kn3, vn3, knt, vnt, kct, vct, sink_b, slope_b):
    _, b, _, w = kct.shape
    assert w == 128 and b == 128, "one lane tile of key positions; new-token columns indexed by lane"
    bs = SWA_SAMPLE_BS
    cache_spec = pl.BlockSpec((1, bs, KV_DIM, w), lambda i: (layer, i, 0, 0))
    return pl.pallas_call(
        _swa_sample_attn_kernel,
        grid=(b // bs,),
        in_specs=[
            pl.BlockSpec((bs, N_HEADS, HEAD_DIM), lambda i: (i, 0, 0)),
            pl.BlockSpec((bs, 1, KV_DIM), lambda i: (i, 0, 0)),
            pl.BlockSpec((bs, 1, KV_DIM), lambda i: (i, 0, 0)),
            _const_spec((KV_DIM, b)),
            _const_spec((KV_DIM, b)),
            cache_spec,
            cache_spec,
            _const_spec((N_HEADS, 128)),
            _const_spec((N_HEADS, 128)),
        ],
        out_specs=[
            pl.BlockSpec((bs, N_HEADS, HEAD_DIM), lambda i: (i, 0, 0)),
            pl.BlockSpec((bs, KV_DIM, w), lambda i: (i, 0, 0)),
            pl.BlockSpec((bs, KV_DIM, w), lambda i: (i, 0, 0)),
        ],
        out_shape=[
            jax.ShapeDtypeStruct((b, N_HEADS, HEAD_DIM), F32),
            jax.ShapeDtypeStruct((b, KV_DIM, w), F32),
            jax.ShapeDtypeStruct((b, KV_DIM, w), F32),
        ],
        compiler_params=_params(1),
        name="swa_sample_attn",
    )(q3, kn3, vn3, knt, vnt, kct, vct, sink_b, slope_b)


def _proj_residual_kernel(x_ref, o_ref_in, wout_ref, y_ref):
    y_ref[...] = x_ref[...] + _dot(o_ref_in[...].astype(BF16), wout_ref[...])


def _proj_residual(x, o, wout):
    b = x.shape[0]
    return pl.pallas_call(
        _proj_residual_kernel,
        grid=(1,),
        in_specs=[_const_spec((b, D_MODEL)), _const_spec((b, Q_DIM)), _const_spec((Q_DIM, D_MODEL))],
        out_specs=pl.BlockSpec((b, D_MODEL), lambda i: (0, 0)),
        out_shape=jax.ShapeDtypeStruct((b, D_MODEL), F32),
        compiler_params=_params(1),
        name="proj_residual",
    )(x, o, wout)


def kernel(x_prompt, x_sample, state_rglru_h, state_rglru_conv, cache_swa_k, cache_swa_v, norm_mix_g, norm_mlp_g, lru_w_in, lru_conv_w, lru_conv_b, lru_w_a, lru_b_a, lru_w_x, lru_b_x, lru_lambda, lru_w_out, attn_w_qkv, attn_q_norm, attn_k_norm, attn_sinks, attn_w_out, mlp_w_up, mlp_w_down):
    n_p, t_p, _ = x_prompt.shape
    n_s = x_sample.shape[0]
    w_buf = cache_swa_k.shape[2]
    heads = jnp.arange(1, N_HEADS + 1, dtype=F32)
    slopes = jnp.exp2(-8.0 * heads / N_HEADS)
    slopes_b = jnp.broadcast_to(slopes[:, None], (N_HEADS, 128))

    to_feature_major = lambda c: c.transpose(0, 1, 3, 4, 2).reshape(c.shape[0], n_s, KV_DIM, w_buf)
    from_feature_major = lambda c: c.reshape(c.shape[0], n_s, N_KV, HEAD_DIM, w_buf).transpose(0, 1, 4, 2, 3)
    kct, vct = to_feature_major(cache_swa_k), to_feature_major(cache_swa_v)

    wax_f32 = jnp.concatenate([lru_w_a, lru_w_x], axis=-1).reshape(N_A, D_RNN, 2 * LRU_BW)
    wkvt = attn_w_qkv[:, :, Q_DIM:].transpose(0, 2, 1).astype(BF16)

    def weight_sources(layer):
        mixer = ([(lru_w_in, layer // 2), (wax_f32, layer // 2), (lru_w_out, layer // 2)] if layer % 2 == 0 else
                 [(attn_w_qkv, layer // 2), (attn_w_out, layer // 2)])
        return mixer + [(mlp_w_up, layer), (mlp_w_down, layer)]

    weights = [w[layer].astype(BF16) for w, layer in weight_sources(0)]

    row = lambda v: v.reshape(1, -1)
    yp = x_prompt
    ys = x_sample.reshape(n_s, D_MODEL)
    h_p, c_p, k_p, v_p, h_s, c_s, k_s, v_s = ([] for _ in range(8))
    for layer in range(DEPTH):
        j = layer // 2
        g_mix = row(norm_mix_g[layer])
        g_mlp = row(norm_mlp_g[layer])
        casts = weight_sources(layer + 1) if layer + 1 < DEPTH else []
        *mixer_w, wu, wd = weights
        if layer % 2 == 0:
            win, wax, wlo = mixer_w
            args = (g_mix, win, lru_conv_w[j], row(lru_conv_b[j]), wax.reshape(LRU_BLOCKS, LRU_BW, 2 * LRU_BW),
                    row(lru_b_a[j]), row(lru_b_x[j]), row(lru_lambda[j]), wlo)
            ys, hs, cs = _lru_sample(ys, state_rglru_h[j], state_rglru_conv[j].transpose(1, 0, 2), *args)
            yp, ys, hp, cp, *weights = _lru_mlp(yp, ys, *args, g_mlp, wu, wd, casts)
            h_p.append(hp.reshape(n_p, D_RNN)); c_p.append(cp)
            h_s.append(hs); c_s.append(cs.transpose(1, 0, 2))
        else:
            wqkv, wao = mixer_w
            qg, kg = row(attn_q_norm[j]), row(attn_k_norm[j])
            sinks = attn_sinks[j]
            kgt = jnp.broadcast_to(attn_k_norm[j][:, None], (HEAD_DIM, n_s))
            q, kn, vn, knt, vnt = _swa_sample_qkv(ys, g_mix, wqkv, wkvt[j], qg, kg, kgt)
            o3, ks, vs = _swa_sample_attn(
                j, q.reshape(n_s, N_HEADS, HEAD_DIM), kn.reshape(n_s, 1, KV_DIM), vn.reshape(n_s, 1, KV_DIM),
                knt, vnt, kct, vct, jnp.broadcast_to(sinks[:, None], (N_HEADS, 128)), slopes_b)
            k_s.append(ks); v_s.append(vs)
            ys = _proj_residual(ys, o3.reshape(n_s, Q_DIM), wao)
            stack = (k_s, v_s) if len(k_s) == N_B else ()
            yp, ys, kp, vp, *rest = _swa_mlp(
                yp, ys, sinks, slopes, g_mix, wqkv, jnp.tile(qg, (1, GROUP)), jnp.tile(kg, (1, N_KV)),
                wao, g_mlp, wu, wd, stack, casts)
            weights, stacked = rest[:len(casts)], rest[len(casts):]
            k_p.append(kp.reshape(n_p, w_buf, N_KV, HEAD_DIM)); v_p.append(vp.reshape(n_p, w_buf, N_KV, HEAD_DIM))
    return (yp, ys.reshape(n_s, 1, D_MODEL),
            jnp.stack(h_p), jnp.stack(c_p), jnp.stack(k_p), jnp.stack(v_p),
            jnp.stack(h_s), jnp.stack(c_s), from_feature_major(stacked[0]), from_feature_major(stacked[1]))
```

```python
import functools

import jax
import jax.numpy as jnp
from jax import lax
from jax.experimental import pallas as pl
from jax.experimental.pallas import tpu as pltpu

F32 = jnp.float32
BF16 = jnp.bfloat16

D_MODEL = 1024
D_RNN = 1024
D_FF = 4096
DEPTH = 4
N_A = (DEPTH + 1) // 2
N_B = DEPTH // 2
CONV_W = 4
LRU_BLOCKS = 4
LRU_BW = D_RNN // LRU_BLOCKS
LRU_C = 8.0
N_HEADS = 16
N_KV = 4
GROUP = N_HEADS // N_KV
HEAD_DIM = 64
Q_DIM = N_HEADS * HEAD_DIM
KV_DIM = N_KV * HEAD_DIM
QKV_DIM = Q_DIM + 2 * KV_DIM
WINDOW = 128
assert GROUP * HEAD_DIM == KV_DIM and N_KV % 2 == 0 and GROUP % 2 == 0
RMS_EPS = 1e-6
NEG_INF = -1e30

V7X_VMEM_LIMIT_BYTES = 56 * 1024 * 1024
SUBLANES = 8
BF16_SUBLANES = 16
CAST_BLOCKS = 64
STACK_BLOCKS = 64

MLP_FC = 1024
MLP_LEAD = 3
LRU_TM = 256
SWA_TM = 256
SWA_SAMPLE_BS = 16
SWA_SAMPLE_UNROLL = 8


def _params(n_axes):
    return pltpu.CompilerParams(
        dimension_semantics=("arbitrary",) * n_axes,
        vmem_limit_bytes=V7X_VMEM_LIMIT_BYTES,
    )


def _const_spec(shape):
    nd = len(shape)
    return pl.BlockSpec(shape, lambda *_: (0,) * nd, pipeline_mode=pl.Buffered(1))


def _rms(x, g):
    return x * lax.rsqrt(jnp.mean(x * x, axis=-1, keepdims=True) + RMS_EPS) * g


def _dot(a, b):
    return jnp.dot(a, b, preferred_element_type=F32)


def _dot_nt(a, b):
    return lax.dot_general(a, b, (((1,), (1,)), ((), ())), preferred_element_type=F32)


def _cast_specs(sources):
    block = lambda i: jnp.minimum(i, CAST_BLOCKS - 1)
    in_specs, out_specs, out_shapes = [], [], []
    for w, layer in sources:
        _, r, c = w.shape
        rb = r // CAST_BLOCKS
        assert rb * CAST_BLOCKS == r and rb % BF16_SUBLANES == 0
        in_specs.append(pl.BlockSpec((None, rb, c), lambda i, layer=layer: (layer, block(i), 0)))
        out_specs.append(pl.BlockSpec((rb, c), lambda i: (block(i), 0)))
        out_shapes.append(jax.ShapeDtypeStruct((r, c), BF16))
    return in_specs, out_specs, out_shapes


def _stack_specs(groups):
    block = lambda i: jnp.minimum(i, STACK_BLOCKS - 1)
    in_specs, out_specs, out_shapes = [], [], []
    for group in groups:
        shape, n = group[0].shape, len(group)
        rb, rest = shape[0] // STACK_BLOCKS, shape[1:]
        assert rb * STACK_BLOCKS == shape[0]
        zeros = (0,) * len(rest)
        in_specs += [pl.BlockSpec((rb,) + rest, lambda i: (block(i),) + zeros)] * n
        out_specs.append(pl.BlockSpec((n, rb) + rest, lambda i: (0, block(i)) + zeros))
        out_shapes.append(jax.ShapeDtypeStruct((n,) + shape, group[0].dtype))
    return in_specs, out_specs, out_shapes


def _with_side_jobs(kernel, n_in, n_out, n_cast, group_sizes, *refs):
    n_parts = sum(group_sizes)
    ins, rest = refs[:n_in], refs[n_in:]
    src, parts, rest = rest[:n_cast], rest[n_cast:n_cast + n_parts], rest[n_cast + n_parts:]
    outs, dst, stacked = rest[:n_out], rest[n_out:n_out + n_cast], rest[n_out + n_cast:n_out + n_cast + len(group_sizes)]
    kernel(*ins, *outs, *rest[n_out + n_cast + len(group_sizes):])
    for s_ref, d_ref in zip(src, dst):
        d_ref[...] = s_ref[...].astype(BF16)
    for d_ref, size in zip(stacked, group_sizes):
        for l in range(size):
            d_ref[l] = parts[l][...]
        parts = parts[size:]


def _chunk_maps(n, nj):
    total = n * nj
    c_in = lambda i: jnp.minimum(i, total - 1)
    c_out = lambda i: jnp.clip(i - 1, 0, total - 1)
    x_map = lambda i: (c_in(i) // nj, lax.rem(c_in(i), nj), 0)
    o_map = lambda i: (c_out(i) // nj, lax.rem(c_out(i), nj), 0)
    seq_map = lambda i: (c_in(i) // nj, 0, 0)
    return x_map, o_map, seq_map


def _softplus(x):
    return jnp.maximum(x, 0.0) + jnp.log1p(jnp.exp(-jnp.abs(x)))


def _lru_gates(xc_k, ga, ba_k, bx_k, sp_k):
    r = jax.nn.sigmoid(ga[:, :LRU_BW] + ba_k)
    i = jax.nn.sigmoid(ga[:, LRU_BW:] + bx_k)
    log_a = -LRU_C * r * sp_k
    a = jnp.exp(log_a)
    u = jnp.sqrt(-jnp.tanh(log_a) * (a * a + 1.0)) * (i * xc_k)
    return a, u


def _scan_rows(a, u, h0):
    rows, c = a.shape
    groups = rows // SUBLANES
    a = a.reshape(groups, SUBLANES, c)
    u = u.reshape(groups, SUBLANES, c)
    row = lax.broadcasted_iota(jnp.int32, a.shape, 1)
    d = 1
    while d < SUBLANES:
        keep = row >= d
        u = jnp.where(keep, u + a * pltpu.roll(u, d, 1), u)
        a = jnp.where(keep, a * pltpu.roll(a, d, 1), a)
        d *= 2
    hs = []
    h = h0
    for g in range(groups):
        hg = a[g] * h + u[g]
        hs.append(hg)
        h = hg[SUBLANES - 1:SUBLANES, :]
    return jnp.concatenate(hs, axis=0)


def _lru_mlp_kernel(n_seq, x_ref, xs_ref, g_ref, win_ref, cw_ref, cb_ref, wax_ref, ba_ref, bx_ref, lam_ref,
                    wout_ref, g2_ref, wu_ref, wd_ref, o_ref, os_ref, h_ref, c_ref, xbuf, hcar, ymid):
    tm = x_ref.shape[1]
    total = pl.num_programs(0) - 2
    nj = total // n_seq
    i = pl.program_id(0)
    j = lax.rem(i, nj)

    def mlp_chunk(xb, c):
        h = _dot(xb, wu_ref[:, c * MLP_FC:(c + 1) * MLP_FC])
        return jnp.square(jnp.maximum(h, 0.0)).astype(BF16)

    def step(do_mixer, do_mlp):
        if do_mlp:
            yprev = ymid[...]
            yb = _rms(yprev, g2_ref[...]).astype(BF16)
            mlp = yprev
        if not do_mixer:
            for c in range(D_FF // MLP_FC):
                mlp = mlp + _dot(mlp_chunk(yb, c), wd_ref[c * MLP_FC:(c + 1) * MLP_FC, :])
            o_ref[0] = mlp
            return
        x = x_ref[0]
        xb = _rms(x, g_ref[...]).astype(BF16)
        u_in = _dot(xb, win_ref[...])
        hidden = {c: mlp_chunk(yb, c) for c in range(MLP_LEAD)} if do_mlp else {}
        xr = u_in[:, D_RNN:]
        xbuf[SUBLANES:SUBLANES + tm, :] = xr
        cw = cw_ref[...]
        xc = xbuf[pl.ds(SUBLANES - 3, tm), :] * cw[0:1, :]
        xc = xc + xbuf[pl.ds(SUBLANES - 2, tm), :] * cw[1:2, :]
        xc = xc + xbuf[pl.ds(SUBLANES - 1, tm), :] * cw[2:3, :]
        xc = xc + xr * cw[3:4, :]
        xc = xc + cb_ref[...]
        xbuf[0:SUBLANES, :] = xbuf[tm:tm + SUBLANES, :]

        sp = _softplus(-lam_ref[...])
        blocks = [slice(k * LRU_BW, (k + 1) * LRU_BW) for k in range(LRU_BLOCKS)]
        gates = [_dot(xc[:, cs].astype(BF16), wax_ref[k]) for k, cs in enumerate(blocks)]
        acc = x
        for k, cs in enumerate(blocks):
            if do_mlp and k + MLP_LEAD < LRU_BLOCKS:
                hidden[k + MLP_LEAD] = mlp_chunk(yb, k + MLP_LEAD)
            a, u = _lru_gates(xc[:, cs], gates[k], ba_ref[:, cs], bx_ref[:, cs], sp[:, cs])
            h = _scan_rows(a, u, hcar[:, cs])
            hcar[:, cs] = h[tm - 1:tm, :]
            y = h * jax.nn.gelu(u_in[:, cs])
            if do_mlp:
                mlp = mlp + _dot(hidden.pop(k), wd_ref[k * MLP_FC:(k + 1) * MLP_FC, :])
            acc = acc + _dot(y.astype(BF16), wout_ref[cs, :])
        ymid[...] = acc
        if do_mlp:
            o_ref[0] = mlp

    mixing = i < total

    @pl.when(mixing & (j == 0))
    def _():
        xbuf[0:SUBLANES, :] = jnp.zeros((SUBLANES, D_RNN), F32)
        hcar[...] = jnp.zeros_like(hcar)

    pl.when(i == 0)(lambda: step(True, False))
    pl.when(mixing & (i > 0))(lambda: step(True, True))
    pl.when(i == total)(lambda: step(False, True))

    @pl.when(mixing & (j == nj - 1))
    def _():
        h_ref[0] = hcar[...]
        c_ref[0] = xbuf[pl.ds(SUBLANES - (CONV_W - 1), CONV_W - 1), :]

    @pl.when(i == total + 1)
    def _():
        x = xs_ref[...]
        xb = _rms(x, g2_ref[...]).astype(BF16)
        acc = x
        for c in range(D_FF // MLP_FC):
            acc = acc + _dot(mlp_chunk(xb, c), wd_ref[c * MLP_FC:(c + 1) * MLP_FC, :])
        os_ref[...] = acc


def _lru_mlp(x, xs, g, win, cw, cb, wax, ba, bx, lam, wout, g2, wu, wd, casts=()):
    n, t, _ = x.shape
    rows_s = xs.shape[0]
    tm = LRU_TM
    assert D_FF // MLP_FC == LRU_BLOCKS
    nj = t // tm
    n_steps = n * nj + 2
    x_map, o_map, seq_map = _chunk_maps(n, nj)
    sample_spec = pl.BlockSpec((rows_s, D_MODEL), lambda i: (0, 0))
    assert n_steps >= CAST_BLOCKS
    cast_in, cast_out, cast_shapes = _cast_specs(casts)
    return pl.pallas_call(
        functools.partial(_with_side_jobs, functools.partial(_lru_mlp_kernel, n), 14, 4, len(casts), ()),
        grid=(n_steps,),
        in_specs=[
            pl.BlockSpec((1, tm, D_MODEL), x_map),
            sample_spec,
            _const_spec((1, D_MODEL)),
            _const_spec((D_MODEL, 2 * D_RNN)),
            _const_spec((CONV_W, D_RNN)),
            _const_spec((1, D_RNN)),
            _const_spec((LRU_BLOCKS, LRU_BW, 2 * LRU_BW)),
            _const_spec((1, D_RNN)),
            _const_spec((1, D_RNN)),
            _const_spec((1, D_RNN)),
            _const_spec((D_RNN, D_MODEL)),
            _const_spec((1, D_MODEL)),
            _const_spec((D_MODEL, D_FF)),
            _const_spec((D_FF, D_MODEL)),
        ] + cast_in,
        out_specs=[
            pl.BlockSpec((1, tm, D_MODEL), o_map),
            sample_spec,
            pl.BlockSpec((1, 1, D_RNN), seq_map),
            pl.BlockSpec((1, CONV_W - 1, D_RNN), seq_map),
        ] + cast_out,
        out_shape=[
            jax.ShapeDtypeStruct((n, t, D_MODEL), F32),
            jax.ShapeDtypeStruct((rows_s, D_MODEL), F32),
            jax.ShapeDtypeStruct((n, 1, D_RNN), F32),
            jax.ShapeDtypeStruct((n, CONV_W - 1, D_RNN), F32),
        ] + cast_shapes,
        scratch_shapes=[
            pltpu.VMEM((SUBLANES + tm, D_RNN), F32),
            pltpu.VMEM((1, D_RNN), F32),
            pltpu.VMEM((tm, D_MODEL), F32),
        ],
        compiler_params=_params(1),
        name="lru_mlp",
    )(x, xs, g, win, cw, cb, wax, ba, bx, lam, wout, g2, wu, wd, *[w for w, _ in casts])


def _lru_sample_kernel(x_ref, h0_ref, c0_ref, g_ref, win_ref, cw_ref, cb_ref, wax_ref, ba_ref, bx_ref, lam_ref,
                       wout_ref, o_ref, h_ref, c_ref):
    x = x_ref[...]
    xb = _rms(x, g_ref[...]).astype(BF16)
    u_in = _dot(xb, win_ref[...])
    xr = u_in[:, D_RNN:]
    cw = cw_ref[...]
    xc = c0_ref[0] * cw[0:1, :]
    xc = xc + c0_ref[1] * cw[1:2, :]
    xc = xc + c0_ref[2] * cw[2:3, :]
    xc = xc + xr * cw[3:4, :]
    xc = xc + cb_ref[...]
    c_ref[0] = c0_ref[1]
    c_ref[1] = c0_ref[2]
    c_ref[2] = xr

    sp = _softplus(-lam_ref[...])
    acc = x
    for k in range(LRU_BLOCKS):
        cs = slice(k * LRU_BW, (k + 1) * LRU_BW)
        ga = _dot(xc[:, cs].astype(BF16), wax_ref[k])
        a, u = _lru_gates(xc[:, cs], ga, ba_ref[:, cs], bx_ref[:, cs], sp[:, cs])
        h = a * h0_ref[:, cs] + u
        h_ref[:, cs] = h
        y = h * jax.nn.gelu(u_in[:, cs])
        acc = acc + _dot(y.astype(BF16), wout_ref[cs, :])
    o_ref[...] = acc


def _lru_sample(x, h0, c0, g, win, cw, cb, wax, ba, bx, lam, wout):
    b = x.shape[0]
    shapes = [(b, D_MODEL), (b, D_RNN), (CONV_W - 1, b, D_RNN), (1, D_MODEL), (D_MODEL, 2 * D_RNN),
              (CONV_W, D_RNN), (1, D_RNN), (LRU_BLOCKS, LRU_BW, 2 * LRU_BW), (1, D_RNN), (1, D_RNN), (1, D_RNN),
              (D_RNN, D_MODEL)]
    return pl.pallas_call(
        _lru_sample_kernel,
        grid=(1,),
        in_specs=[_const_spec(s) for s in shapes],
        out_specs=[
            pl.BlockSpec((b, D_MODEL), lambda i: (0, 0)),
            pl.BlockSpec((b, D_RNN), lambda i: (0, 0)),
            pl.BlockSpec((CONV_W - 1, b, D_RNN), lambda i: (0, 0, 0)),
        ],
        out_shape=[
            jax.ShapeDtypeStruct((b, D_MODEL), F32),
            jax.ShapeDtypeStruct((b, D_RNN), F32),
            jax.ShapeDtypeStruct((CONV_W - 1, b, D_RNN), F32),
        ],
        compiler_params=_params(1),
        name="lru_sample",
    )(x, h0, c0, g, win, cw, cb, wax, ba, bx, lam, wout)


def _head_norm(z, g):
    return z * lax.rsqrt(jnp.mean(z * z, axis=-1, keepdims=True) + RMS_EPS) * g


def _heads_norm(z, g):
    z2 = z * z
    low = lax.broadcasted_iota(jnp.int32, (z.shape[0], 2 * HEAD_DIM), 1) < HEAD_DIM
    parts = []
    for c in range(z.shape[1] // (2 * HEAD_DIM)):
        zz = z2[:, c * 2 * HEAD_DIM:(c + 1) * 2 * HEAD_DIM]
        s_lo = jnp.sum(jnp.where(low, zz, 0.0), axis=-1, keepdims=True)
        s_hi = jnp.sum(jnp.where(low, 0.0, zz), axis=-1, keepdims=True)
        parts.append(jnp.where(low, s_lo, s_hi))
    ssq = jnp.concatenate(parts, axis=1)
    return z * lax.rsqrt(ssq * (1.0 / HEAD_DIM) + RMS_EPS) * g


_MLP_ORDER = (("up", 0), ("up", 1), ("down", 0), ("up", 2), ("down", 1), ("up", 3), ("down", 2), ("down", 3))


def _swa_mlp_kernel(n_seq, sink_ref, slope_ref, x_ref, xs_ref, g_ref, wqkv_ref, qg_ref, kg_ref,
                    wout_ref, g2_ref, wu_ref, wd_ref, o_ref, os_ref, k_ref, v_ref, k2, v2, qn, obuf, bias, ymid):
    i = pl.program_id(0)
    total = pl.num_programs(0) - 2
    nj = total // n_seq

    def mlp_up(xb, c):
        h = _dot(xb, wu_ref[:, c * MLP_FC:(c + 1) * MLP_FC])
        return jnp.square(jnp.maximum(h, 0.0)).astype(BF16)

    def mlp_down(hk, c):
        return _dot(hk, wd_ref[c * MLP_FC:(c + 1) * MLP_FC, :])

    step = functools.partial(_swa_mlp_step, i == 0, lax.rem(i, nj) == 0, lax.rem(i, nj) == nj - 1, sink_ref, slope_ref, x_ref, g_ref, wqkv_ref, qg_ref, kg_ref,
                             wout_ref, g2_ref, mlp_up, mlp_down, o_ref, k_ref, v_ref, k2, v2, qn, obuf,
                             bias, ymid)
    pl.when(i == 0)(lambda: step(True, False))
    pl.when((i > 0) & (i < total))(lambda: step(True, True))
    pl.when(i == total)(lambda: step(False, True))

    @pl.when(i == total + 1)
    def _():
        x = xs_ref[...]
        xb = _rms(x, g2_ref[...]).astype(BF16)
        acc = x
        for c in range(D_FF // MLP_FC):
            acc = acc + mlp_down(mlp_up(xb, c), c)
        os_ref[...] = acc


def _swa_mlp_step(first, new_seq, seq_ends, sink_ref, slope_ref, x_ref, g_ref, wqkv_ref, qg_ref, kg_ref, wout_ref,
                  g2_ref, mlp_up, mlp_down, o_ref, k_ref, v_ref, k2, v2, qn, obuf, bias, ymid, do_mixer, do_mlp):
    tm = x_ref.shape[1]
    nblk = tm // WINDOW

    if do_mlp:
        yprev = ymid[...]
        yb = _rms(yprev, g2_ref[...]).astype(BF16)
        mlp = yprev
    if not do_mixer:
        for c in range(D_FF // MLP_FC):
            mlp = mlp + mlp_down(mlp_up(yb, c), c)
        o_ref[0] = mlp
        return

    if not do_mlp:
        @pl.when(first)
        def _():
            qi = lax.broadcasted_iota(jnp.int32, (WINDOW, 2 * WINDOW), 0)
            si = lax.broadcasted_iota(jnp.int32, (WINDOW, 2 * WINDOW), 1)
            dist = qi + WINDOW - si
            valid = (dist >= 0) & (dist < WINDOW)
            dist_f = dist.astype(F32)
            for h in range(N_HEADS):
                bias[h] = jnp.where(valid, -(slope_ref[h] * dist_f), NEG_INF)

        k2[:, :, 0:WINDOW, :] = jnp.zeros((N_KV, 2, WINDOW, 2 * HEAD_DIM), BF16)
        v2[:, :, 0:WINDOW, :] = jnp.zeros((N_KV, 2, WINDOW, 2 * HEAD_DIM), BF16)

    x = x_ref[0]
    xb = _rms(x, g_ref[...]).astype(BF16)
    qkv = _dot(xb, wqkv_ref[...])

    kn = _heads_norm(qkv[:, Q_DIM:Q_DIM + KV_DIM], kg_ref[...])
    vv = qkv[:, Q_DIM + KV_DIM:]
    k_ref[0] = kn[tm - WINDOW:, :]
    v_ref[0] = vv[tm - WINDOW:, :]
    low = lax.broadcasted_iota(jnp.int32, (tm, 2 * HEAD_DIM), 1) < HEAD_DIM
    for src, dst in ((kn, k2), (vv, v2)):
        for sl in range(N_KV // 2):
            z = src[:, sl * 2 * HEAD_DIM:(sl + 1) * 2 * HEAD_DIM]
            zs = pltpu.roll(z, HEAD_DIM, 1)
            dst[2 * sl, 0, WINDOW:WINDOW + tm, :] = jnp.where(low, z, 0.0).astype(BF16)
            dst[2 * sl, 1, WINDOW:WINDOW + tm, :] = jnp.where(low, 0.0, zs).astype(BF16)
            dst[2 * sl + 1, 0, WINDOW:WINDOW + tm, :] = jnp.where(low, zs, 0.0).astype(BF16)
            dst[2 * sl + 1, 1, WINDOW:WINDOW + tm, :] = jnp.where(low, 0.0, z).astype(BF16)
    qg = qg_ref[...] * (HEAD_DIM ** -0.5)
    for kv in range(N_KV):
        cs = slice(kv * KV_DIM, (kv + 1) * KV_DIM)
        qn[:, cs] = _heads_norm(qkv[:, cs], qg).astype(BF16)

    col = lax.broadcasted_iota(jnp.int32, (1, 2 * WINDOW), 1)
    no_prev = jnp.where((col < WINDOW) & new_seq, NEG_INF, 0.0)

    def scores(kv, i):
        rq = slice(i * WINDOW, (i + 1) * WINDOW)
        ql = jnp.concatenate([qn[rq, kv * KV_DIM + p * 2 * HEAD_DIM:kv * KV_DIM + (p + 1) * 2 * HEAD_DIM]
                              for p in range(GROUP // 2)], axis=0)
        return [_dot_nt(ql, k2[kv, c, i * WINDOW:(i + 2) * WINDOW, :]) for c in range(2)]

    work = [(kv, i) for kv in range(N_KV) for i in range(nblk)]
    hidden = {}
    s_next = scores(*work[0])
    for n, (kv, i) in enumerate(work):
        mlp_ops = _MLP_ORDER[n * len(_MLP_ORDER) // len(work):(n + 1) * len(_MLP_ORDER) // len(work)]
        for kind, c in mlp_ops if do_mlp else ():
            if kind == "up":
                hidden[c] = mlp_up(yb, c)
            else:
                mlp = mlp + mlp_down(hidden.pop(c), c)
        rq = slice(i * WINDOW, (i + 1) * WINDOW)
        rk = slice(i * WINDOW, (i + 2) * WINDOW)
        s = s_next
        if n + 1 < len(work):
            s_next = scores(*work[n + 1])
        probs = [[None, None], [None, None]]
        for g in range(GROUP):
            p, c = divmod(g, 2)
            h = kv * GROUP + g
            sink = sink_ref[h]
            sg = s[c][p * WINDOW:(p + 1) * WINDOW, :] + bias[h]
            if i == 0:
                sg = sg + no_prev
            m = jnp.maximum(jnp.max(sg, axis=-1, keepdims=True), sink)
            e = jnp.exp(sg - m)
            pr = e / (jnp.sum(e, axis=-1, keepdims=True) + jnp.exp(sink - m))
            probs[c][p] = pr.astype(BF16)
        pv = (_dot(jnp.concatenate(probs[0], axis=0), v2[kv, 0, rk, :])
              + _dot(jnp.concatenate(probs[1], axis=0), v2[kv, 1, rk, :]))
        for p in range(GROUP // 2):
            obuf[rq, kv * KV_DIM + p * 2 * HEAD_DIM:kv * KV_DIM + (p + 1) * 2 * HEAD_DIM] = (
                pv[p * WINDOW:(p + 1) * WINDOW, :].astype(BF16))

    assert not hidden
    zero = jnp.zeros((), BF16)
    k2[:, :, 0:WINDOW, :] = jnp.where(seq_ends, zero, k2[:, :, tm:tm + WINDOW, :])
    v2[:, :, 0:WINDOW, :] = jnp.where(seq_ends, zero, v2[:, :, tm:tm + WINDOW, :])
    ymid[...] = x + _dot(obuf[...], wout_ref[...])
    if do_mlp:
        o_ref[0] = mlp


def _swa_mlp(x, xs, sinks, slopes, g, wqkv, qg4, kg4, wout, g2, wu, wd, stack=(), casts=()):
    n, t, _ = x.shape
    rows_s = xs.shape[0]
    tm = SWA_TM
    nj = t // tm
    n_steps = n * nj + 2
    x_map, o_map, seq_map = _chunk_maps(n, nj)
    sample_spec = pl.BlockSpec((rows_s, D_MODEL), lambda i: (0, 0))
    smem = pl.BlockSpec(memory_space=pltpu.SMEM)
    parts = [p for group in stack for p in group]
    assert n_steps >= max(CAST_BLOCKS, STACK_BLOCKS)
    cast_in, cast_out, cast_shapes = _cast_specs(casts)
    stack_in, stack_out, stack_shapes = _stack_specs(stack)
    return pl.pallas_call(
        functools.partial(_with_side_jobs, functools.partial(_swa_mlp_kernel, n), 12, 4, len(casts),
                          tuple(len(group) for group in stack)),
        grid=(n_steps,),
        in_specs=[
            smem,
            smem,
            pl.BlockSpec((1, tm, D_MODEL), x_map),
            sample_spec,
            _const_spec((1, D_MODEL)),
            _const_spec((D_MODEL, QKV_DIM)),
            _const_spec((1, KV_DIM)),
            _const_spec((1, KV_DIM)),
            _const_spec((Q_DIM, D_MODEL)),
            _const_spec((1, D_MODEL)),
            _const_spec((D_MODEL, D_FF)),
            _const_spec((D_FF, D_MODEL)),
        ] + cast_in + stack_in,
        out_specs=[
            pl.BlockSpec((1, tm, D_MODEL), o_map),
            sample_spec,
            pl.BlockSpec((1, WINDOW, KV_DIM), seq_map),
            pl.BlockSpec((1, WINDOW, KV_DIM), seq_map),
        ] + cast_out + stack_out,
        out_shape=[
            jax.ShapeDtypeStruct((n, t, D_MODEL), F32),
            jax.ShapeDtypeStruct((rows_s, D_MODEL), F32),
            jax.ShapeDtypeStruct((n, WINDOW, KV_DIM), F32),
            jax.ShapeDtypeStruct((n, WINDOW, KV_DIM), F32),
        ] + cast_shapes + stack_shapes,
        scratch_shapes=[
            pltpu.VMEM((N_KV, 2, WINDOW + tm, 2 * HEAD_DIM), BF16),
            pltpu.VMEM((N_KV, 2, WINDOW + tm, 2 * HEAD_DIM), BF16),
            pltpu.VMEM((tm, Q_DIM), BF16),
            pltpu.VMEM((tm, Q_DIM), BF16),
            pltpu.VMEM((N_HEADS, WINDOW, 2 * WINDOW), F32),
            pltpu.VMEM((tm, D_MODEL), F32),
        ],
        compiler_params=_params(1),
        name="swa_mlp",
    )(sinks, slopes, x, xs, g, wqkv, qg4, kg4, wout, g2, wu, wd, *[w for w, _ in casts], *parts)


def _swa_sample_qkv_kernel(x_ref, g_ref, wqkv_ref, wkvt_ref, qg_ref, kg_ref, kgt_ref,
                           q_ref, k_ref, v_ref, kt_ref, vt_ref):
    xb = _rms(x_ref[...], g_ref[...]).astype(BF16)
    qkv = _dot(xb, wqkv_ref[...])
    for h in range(N_HEADS):
        cs = slice(h * HEAD_DIM, (h + 1) * HEAD_DIM)
        q_ref[:, cs] = _head_norm(qkv[:, cs], qg_ref[...])
    for kv in range(N_KV):
        cs = slice(kv * HEAD_DIM, (kv + 1) * HEAD_DIM)
        k_ref[:, cs] = _head_norm(qkv[:, Q_DIM + kv * HEAD_DIM:Q_DIM + (kv + 1) * HEAD_DIM], kg_ref[...])
    v_ref[...] = qkv[:, Q_DIM + KV_DIM:]
    kvt = _dot_nt(wkvt_ref[...], xb)
    for kv in range(N_KV):
        rs = slice(kv * HEAD_DIM, (kv + 1) * HEAD_DIM)
        z = kvt[rs, :]
        kt_ref[rs, :] = z * lax.rsqrt(jnp.mean(z * z, axis=0, keepdims=True) + RMS_EPS) * kgt_ref[...]
    vt_ref[...] = kvt[KV_DIM:, :]


def _swa_sample_qkv(x, g, wqkv, wkvt, qg, kg, kgt):
    b = x.shape[0]
    shapes = [(b, D_MODEL), (1, D_MODEL), (D_MODEL, QKV_DIM), (2 * KV_DIM, D_MODEL), (1, HEAD_DIM), (1, HEAD_DIM),
              (HEAD_DIM, b)]
    out_shapes = [(b, Q_DIM), (b, KV_DIM), (b, KV_DIM), (KV_DIM, b), (KV_DIM, b)]
    return pl.pallas_call(
        _swa_sample_qkv_kernel,
        grid=(1,),
        in_specs=[_const_spec(s) for s in shapes],
        out_specs=[pl.BlockSpec(s, lambda i: (0, 0)) for s in out_shapes],
        out_shape=[jax.ShapeDtypeStruct(s, F32) for s in out_shapes],
        compiler_params=_params(1),
        name="swa_sample_qkv",
    )(x, g, wqkv, wkvt, qg, kg, kgt)


def _swa_sample_attn_kernel(q_ref, kn_ref, vn_ref, knt_ref, vnt_ref, kc_ref, vc_ref, sink_ref, slope_ref,
                            o_ref, ko_ref, vo_ref):
    bs = q_ref.shape[0]
    w = kc_ref.shape[3]
    i = pl.program_id(0)
    hrow = lax.broadcasted_iota(jnp.int32, (N_HEADS, KV_DIM), 0)
    feat = lax.broadcasted_iota(jnp.int32, (N_HEADS, KV_DIM), 1)
    own = (hrow // GROUP) == (feat // HEAD_DIM)
    key = lax.broadcasted_iota(jnp.int32, (N_HEADS, w), 1)
    dist = w - key
    bias = jnp.where(dist < WINDOW, -(slope_ref[:, 0:1] * dist.astype(F32)), NEG_INF)
    sink = sink_ref[:, 0:1]
    lane = lax.broadcasted_iota(jnp.int32, (KV_DIM, w), 1)
    newest = lane == w - 1

    def scores(b):
        q = q_ref[b]
        qt = jnp.concatenate([q] * N_KV, axis=1)
        qm = (jnp.where(own, qt, 0.0) * (HEAD_DIM ** -0.5)).astype(BF16)
        return qm, _dot(qm, kc_ref[0, b].astype(BF16)) + bias

    def softmax(b, qm, s):
        kn = kn_ref[b].astype(BF16).astype(F32)
        s_new = jnp.sum(qm.astype(F32) * kn, axis=-1, keepdims=True)
        m = jnp.maximum(jnp.maximum(jnp.max(s, axis=-1, keepdims=True), s_new), sink)
        e = jnp.exp(s - m)
        e_new = jnp.exp(s_new - m)
        den = jnp.sum(e, axis=-1, keepdims=True) + e_new + jnp.exp(sink - m)
        return (e / den).astype(BF16), (e_new / den).astype(BF16).astype(F32)

    def values(b, p, p_new):
        vn = vn_ref[b].astype(BF16).astype(F32)
        return _dot_nt(p, vc_ref[0, b].astype(BF16)) + p_new * vn

    def emit(b, res):
        res = jnp.where(own, res, 0.0)
        o = res[:, 0:HEAD_DIM]
        for kv in range(1, N_KV):
            o = o + res[:, kv * HEAD_DIM:(kv + 1) * HEAD_DIM]
        o_ref[b] = o

    def shift_in(b):
        mine = lane == i * bs + b
        kcol = jnp.sum(jnp.where(mine, knt_ref[...], 0.0), axis=1, keepdims=True)
        vcol = jnp.sum(jnp.where(mine, vnt_ref[...], 0.0), axis=1, keepdims=True)
        ko_ref[b] = jnp.where(newest, kcol, pltpu.roll(kc_ref[0, b], w - 1, 1))
        vo_ref[b] = jnp.where(newest, vcol, pltpu.roll(vc_ref[0, b], w - 1, 1))

    def body(grp, carry):
        seqs = [grp * SWA_SAMPLE_UNROLL + u for u in range(SWA_SAMPLE_UNROLL)]
        staged = [scores(b) for b in seqs]
        probs = [softmax(b, qm, s) for b, (qm, s) in zip(seqs, staged)]
        for b in seqs:
            shift_in(b)
        outs = [values(b, p, p_new) for b, (p, p_new) in zip(seqs, probs)]
        for b, res in zip(seqs, outs):
            emit(b, res)
        return carry

    lax.fori_loop(0, bs // SWA_SAMPLE_UNROLL, body, 0)


def _swa_sample_attn(layer, q3, kn3, vn3, knt, vnt, kct, vct, sink_b, slope_b):
    _, b, _, w = kct.shape
    assert w == 128 and b == 128, "one lane tile of key positions; new-token columns indexed by lane"
    bs = SWA_SAMPLE_BS
    cache_spec = pl.BlockSpec((1, bs, KV_DIM, w), lambda i: (layer, i, 0, 0))
    return pl.pallas_call(
        _swa_sample_attn_kernel,
        grid=(b // bs,),
        in_specs=[
            pl.BlockSpec((bs, N_HEADS, HEAD_DIM), lambda i: (i, 0, 0)),
            pl.BlockSpec((bs, 1, KV_DIM), lambda i: (i, 0, 0)),
            pl.BlockSpec((bs, 1, KV_DIM), lambda i: (i, 0, 0)),
            _const_spec((KV_DIM, b)),
            _const_spec((KV_DIM, b)),
            cache_spec,
            cache_spec,
            _const_spec((N_HEADS, 128)),
            _const_spec((N_HEADS, 128)),
        ],
        out_specs=[
            pl.BlockSpec((bs, N_HEADS, HEAD_DIM), lambda i: (i, 0, 0)),
            pl.BlockSpec((bs, KV_DIM, w), lambda i: (i, 0, 0)),
            pl.BlockSpec((bs, KV_DIM, w), lambda i: (i, 0, 0)),
        ],
        out_shape=[
            jax.ShapeDtypeStruct((b, N_HEADS, HEAD_DIM), F32),
            jax.ShapeDtypeStruct((b, KV_DIM, w), F32),
            jax.ShapeDtypeStruct((b, KV_DIM, w), F32),
        ],
        compiler_params=_params(1),
        name="swa_sample_attn",
    )(q3, kn3, vn3, knt, vnt, kct, vct, sink_b, slope_b)


def _proj_residual_kernel(x_ref, o_ref_in, wout_ref, y_ref):
    y_ref[...] = x_ref[...] + _dot(o_ref_in[...].astype(BF16), wout_ref[...])


def _proj_residual(x, o, wout):
    b = x.shape[0]
    return pl.pallas_call(
        _proj_residual_kernel,
        grid=(1,),
        in_specs=[_const_spec((b, D_MODEL)), _const_spec((b, Q_DIM)), _const_spec((Q_DIM, D_MODEL))],
        out_specs=pl.BlockSpec((b, D_MODEL), lambda i: (0, 0)),
        out_shape=jax.ShapeDtypeStruct((b, D_MODEL), F32),
        compiler_params=_params(1),
        name="proj_residual",
    )(x, o, wout)


def kernel(x_prompt, x_sample, state_rglru_h, state_rglru_conv, cache_swa_k, cache_swa_v, norm_mix_g, norm_mlp_g, lru_w_in, lru_conv_w, lru_conv_b, lru_w_a, lru_b_a, lru_w_x, lru_b_x, lru_lambda, lru_w_out, attn_w_qkv, attn_q_norm, attn_k_norm, attn_sinks, attn_w_out, mlp_w_up, mlp_w_down):
    n_p, t_p, _ = x_prompt.shape
    n_s = x_sample.shape[0]
    w_buf = cache_swa_k.shape[2]
    heads = jnp.arange(1, N_HEADS + 1, dtype=F32)
    slopes = jnp.exp2(-8.0 * heads / N_HEADS)
    slopes_b = jnp.broadcast_to(slopes[:, None], (N_HEADS, 128))

    to_feature_major = lambda c: c.transpose(0, 1, 3, 4, 2).reshape(c.shape[0], n_s, KV_DIM, w_buf)
    from_feature_major = lambda c: c.reshape(c.shape[0], n_s, N_KV, HEAD_DIM, w_buf).transpose(0, 1, 4, 2, 3)
    kct, vct = to_feature_major(cache_swa_k), to_feature_major(cache_swa_v)

    wax_f32 = jnp.concatenate([lru_w_a, lru_w_x], axis=-1).reshape(N_A, D_RNN, 2 * LRU_BW)
    wkvt = attn_w_qkv[:, :, Q_DIM:].transpose(0, 2, 1).astype(BF16)

    def weight_sources(layer):
        mixer = ([(lru_w_in, layer // 2), (wax_f32, layer // 2), (lru_w_out, layer // 2)] if layer % 2 == 0 else
                 [(attn_w_qkv, layer // 2), (attn_w_out, layer // 2)])
        return mixer + [(mlp_w_up, layer), (mlp_w_down, layer)]

    weights = [w[layer].astype(BF16) for w, layer in weight_sources(0)]

    row = lambda v: v.reshape(1, -1)
    yp = x_prompt
    ys = x_sample.reshape(n_s, D_MODEL)
    h_p, c_p, k_p, v_p, h_s, c_s, k_s, v_s = ([] for _ in range(8))
    for layer in range(DEPTH):
        j = layer // 2
        g_mix = row(norm_mix_g[layer])
        g_mlp = row(norm_mlp_g[layer])
        casts = weight_sources(layer + 1) if layer + 1 < DEPTH else []
        *mixer_w, wu, wd = weights
        if layer % 2 == 0:
            win, wax, wlo = mixer_w
            args = (g_mix, win, lru_conv_w[j], row(lru_conv_b[j]), wax.reshape(LRU_BLOCKS, LRU_BW, 2 * LRU_BW),
                    row(lru_b_a[j]), row(lru_b_x[j]), row(lru_lambda[j]), wlo)
            ys, hs, cs = _lru_sample(ys, state_rglru_h[j], state_rglru_conv[j].transpose(1, 0, 2), *args)
            yp, ys, hp, cp, *weights = _lru_mlp(yp, ys, *args, g_mlp, wu, wd, casts)
            h_p.append(hp.reshape(n_p, D_RNN)); c_p.append(cp)
            h_s.append(hs); c_s.append(cs.transpose(1, 0, 2))
        else:
            wqkv, wao = mixer_w
            qg, kg = row(attn_q_norm[j]), row(attn_k_norm[j])
            sinks = attn_sinks[j]
            kgt = jnp.broadcast_to(attn_k_norm[j][:, None], (HEAD_DIM, n_s))
            q, kn, vn, knt, vnt = _swa_sample_qkv(ys, g_mix, wqkv, wkvt[j], qg, kg, kgt)
            o3, ks, vs = _swa_sample_attn(
                j, q.reshape(n_s, N_HEADS, HEAD_DIM), kn.reshape(n_s, 1, KV_DIM), vn.reshape(n_s, 1, KV_DIM),
                knt, vnt, kct, vct, jnp.broadcast_to(sinks[:, None], (N_HEADS, 128)), slopes_b)
            k_s.append(ks); v_s.append(vs)
            ys = _proj_residual(ys, o3.reshape(n_s, Q_DIM), wao)
            stack = (k_s, v_s) if len(k_s) == N_B else ()
            yp, ys, kp, vp, *rest = _swa_mlp(
                yp, ys, sinks, slopes, g_mix, wqkv, jnp.tile(qg, (1, GROUP)), jnp.tile(kg, (1, N_KV)),
                wao, g_mlp, wu, wd, stack, casts)
            weights, stacked = rest[:len(casts)], rest[len(casts):]
            k_p.append(kp.reshape(n_p, w_buf, N_KV, HEAD_DIM)); v_p.append(vp.reshape(n_p, w_buf, N_KV, HEAD_DIM))
    return (yp, ys.reshape(n_s, 1, D_MODEL),
            jnp.stack(h_p), jnp.stack(c_p), jnp.stack(k_p), jnp.stack(v_p),
            jnp.stack(h_s), jnp.stack(c_s), from_feature_major(stacked[0]), from_feature_major(stacked[1]))
```
